```python
import math
import jax, jax.numpy as jnp
from jax import lax
import numpy as np

D_MODEL = 1024
BATCH = 4
SEQ = 4096
DEPTH = 4
DEC_BATCH = 8
DEC_SEQ = 4096
PAST_LEN = 128

N_MIXERS = 2
N_POOL_LAYERS = (DEPTH + 1) // 2
N_ATTN_LAYERS = DEPTH // 2
POOL_WINDOWS = (2, 4, 8, 16)
N_POOL_GROUPS = len(POOL_WINDOWS)
POOL_GROUP_DIM = D_MODEL // N_POOL_GROUPS
HEAD_DIM = 128
N_HEADS = D_MODEL // HEAD_DIM
N_KV_HEADS = 2
GQA_GROUP = N_HEADS // N_KV_HEADS
QKV_DIM = (N_HEADS + 2 * N_KV_HEADS) * HEAD_DIM
ROPE_AXIS_DIM = HEAD_DIM // 2
ROPE_THETA = 10000.0
GRID_W = 64
Q_BLOCK = 128
D_FF = int(math.ceil(8 * D_MODEL / 3 / 256) * 256)
EPS = 1e-6

kernel_name = "hybrid_pool_gqa_axialrope_encoder"


def rms_norm(x, g):
    xf = x.astype(jnp.float32)
    y = xf * lax.rsqrt(jnp.mean(xf * xf, axis=-1, keepdims=True) + EPS)
    return (y * g.astype(jnp.float32)).astype(x.dtype)


def pool_mixer(h, w_pool, b_pool, scale):
    B, T, D = h.shape
    hf = h.astype(jnp.float32)
    cs = jnp.concatenate([jnp.zeros((B, 1, D), jnp.float32), jnp.cumsum(hf, axis=1)], axis=1)
    t = jnp.arange(T, dtype=jnp.int32)
    outs = []
    for gi, w in enumerate(POOL_WINDOWS):
        sl = slice(gi * POOL_GROUP_DIM, (gi + 1) * POOL_GROUP_DIM)
        lo = jnp.maximum(t - w // 2, 0)
        hi = jnp.minimum(t + w // 2, T)
        csg = cs[:, :, sl]
        window_sum = jnp.take(csg, hi, axis=1) - jnp.take(csg, lo, axis=1)
        count = (hi - lo).astype(jnp.float32)[None, :, None]
        pooled = window_sum / count - hf[:, :, sl]
        y = jnp.einsum('btc,cd->btd', pooled, w_pool[gi].astype(jnp.float32)) + b_pool[gi].astype(jnp.float32)
        outs.append(y)
    y = jnp.concatenate(outs, axis=-1) * scale.astype(jnp.float32)
    return y.astype(h.dtype)


def axial_rope_tables(T):
    rows = T // GRID_W
    rr, cc = jnp.meshgrid(jnp.arange(rows, dtype=jnp.float32), jnp.arange(GRID_W, dtype=jnp.float32), indexing='ij')
    row = rr.reshape(-1)
    col = cc.reshape(-1)
    inv_freq = ROPE_THETA ** (-jnp.arange(0, ROPE_AXIS_DIM, 2, dtype=jnp.float32) / ROPE_AXIS_DIM)
    ang_r = row[:, None] * inv_freq[None, :]
    ang_c = col[:, None] * inv_freq[None, :]
    ang_r = jnp.concatenate([ang_r, ang_r], axis=-1)
    ang_c = jnp.concatenate([ang_c, ang_c], axis=-1)
    return jnp.cos(ang_r), jnp.sin(ang_r), jnp.cos(ang_c), jnp.sin(ang_c)


def rotate_half(x):
    x1, x2 = jnp.split(x, 2, axis=-1)
    return jnp.concatenate([-x2, x1], axis=-1)


def apply_axial_rope(x, tables):
    cos_r, sin_r, cos_c, sin_c = tables
    xf = x.astype(jnp.float32)
    xr, xc = xf[..., :ROPE_AXIS_DIM], xf[..., ROPE_AXIS_DIM:]
    cr, sr = cos_r[None, :, None, :], sin_r[None, :, None, :]
    cc, sc = cos_c[None, :, None, :], sin_c[None, :, None, :]
    yr = xr * cr + rotate_half(xr) * sr
    yc = xc * cc + rotate_half(xc) * sc
    return jnp.concatenate([yr, yc], axis=-1).astype(x.dtype)


def attention_mixer(h, w_qkv, q_gain, k_gain, w_o):
    B, T, D = h.shape
    qkv = jnp.einsum('btd,de->bte', h, w_qkv)
    q = qkv[..., :N_HEADS * HEAD_DIM].reshape(B, T, N_HEADS, HEAD_DIM)
    k = qkv[..., N_HEADS * HEAD_DIM:(N_HEADS + N_KV_HEADS) * HEAD_DIM].reshape(B, T, N_KV_HEADS, HEAD_DIM)
    v = qkv[..., (N_HEADS + N_KV_HEADS) * HEAD_DIM:].reshape(B, T, N_KV_HEADS, HEAD_DIM)
    q = rms_norm(q, q_gain)
    k = rms_norm(k, k_gain)
    tables = axial_rope_tables(T)
    q = apply_axial_rope(q, tables)
    k = apply_axial_rope(k, tables)
    n_blk = T // Q_BLOCK
    qb = q.reshape(B, n_blk, Q_BLOCK, N_KV_HEADS, GQA_GROUP, HEAD_DIM).transpose(1, 0, 2, 3, 4, 5)
    scale = 1.0 / math.sqrt(HEAD_DIM)

    def one_block(q_blk):
        s = jnp.einsum('bqkgd,bskd->bkgqs', q_blk, k).astype(jnp.float32) * scale
        p = jax.nn.softmax(s, axis=-1)
        return jnp.einsum('bkgqs,bskd->bqkgd', p.astype(v.dtype), v)

    o = lax.map(one_block, qb)
    o = o.transpose(1, 0, 2, 3, 4, 5).reshape(B, T, N_HEADS * HEAD_DIM)
    return jnp.einsum('bte,ed->btd', o, w_o)


def swiglu_ffn(h, w_gate_up, w_down):
    gu = jnp.einsum('btd,df->btf', h, w_gate_up)
    g, u = gu[..., :D_FF], gu[..., D_FF:]
    return jnp.einsum('btf,fd->btd', jax.nn.silu(g) * u, w_down)


def trunk(x, norm_mix, norm_ffn, pool_w, pool_b, pool_scale,
          attn_w_qkv, attn_q_gain, attn_k_gain, attn_w_o, ffn_w_gate_up, ffn_w_down):
    for i in range(DEPTH):
        h = rms_norm(x, norm_mix[i])
        j = i // N_MIXERS
        if i % N_MIXERS == 0:
            x = x + pool_mixer(h, pool_w[j], pool_b[j], pool_scale[j])
        else:
            x = x + attention_mixer(h, attn_w_qkv[j], attn_q_gain[j], attn_k_gain[j], attn_w_o[j])
        h = rms_norm(x, norm_ffn[i])
        x = x + swiglu_ffn(h, ffn_w_gate_up[i], ffn_w_down[i])
    return x


def setup_inputs(seed: int = 0) -> dict:
    key = jax.random.key(seed)
    ks = jax.random.split(key, 16)
    f32 = jnp.float32
    nrm = lambda k, shape, s: jax.random.normal(k, shape, f32) * s
    return {
        "x_prompt": nrm(ks[0], (BATCH, SEQ, D_MODEL), 1.0),
        "x_sample": nrm(ks[1], (DEC_BATCH, DEC_SEQ, D_MODEL), 1.0),
        "norm_mix": 1.0 + nrm(ks[2], (DEPTH, D_MODEL), 0.05),
        "norm_ffn": 1.0 + nrm(ks[3], (DEPTH, D_MODEL), 0.05),
        "pool_w": nrm(ks[4], (N_POOL_LAYERS, N_POOL_GROUPS, POOL_GROUP_DIM, POOL_GROUP_DIM), POOL_GROUP_DIM ** -0.5),
        "pool_b": nrm(ks[5], (N_POOL_LAYERS, N_POOL_GROUPS, POOL_GROUP_DIM), 0.02),
        "pool_scale": 1.0 + nrm(ks[6], (N_POOL_LAYERS, D_MODEL), 0.1),
        "attn_w_qkv": nrm(ks[7], (N_ATTN_LAYERS, D_MODEL, QKV_DIM), D_MODEL ** -0.5),
        "attn_q_gain": 1.0 + nrm(ks[8], (N_ATTN_LAYERS, HEAD_DIM), 0.05),
        "attn_k_gain": 1.0 + nrm(ks[9], (N_ATTN_LAYERS, HEAD_DIM), 0.05),
        "attn_w_o": nrm(ks[10], (N_ATTN_LAYERS, N_HEADS * HEAD_DIM, D_MODEL), (N_HEADS * HEAD_DIM) ** -0.5),
        "ffn_w_gate_up": nrm(ks[11], (DEPTH, D_MODEL, 2 * D_FF), D_MODEL ** -0.5),
        "ffn_w_down": nrm(ks[12], (DEPTH, D_FF, D_MODEL), D_FF ** -0.5),
    }


def reference(x_prompt, x_sample, norm_mix, norm_ffn, pool_w, pool_b, pool_scale,
              attn_w_qkv, attn_q_gain, attn_k_gain, attn_w_o, ffn_w_gate_up, ffn_w_down):
    y_prompt = trunk(x_prompt, norm_mix, norm_ffn, pool_w, pool_b, pool_scale,
                     attn_w_qkv, attn_q_gain, attn_k_gain, attn_w_o, ffn_w_gate_up, ffn_w_down)
    y_sample = trunk(x_sample, norm_mix, norm_ffn, pool_w, pool_b, pool_scale,
                     attn_w_qkv, attn_q_gain, attn_k_gain, attn_w_o, ffn_w_gate_up, ffn_w_down)
    return (y_prompt, y_sample)
```

```python
import functools
import math

import jax
import jax.numpy as jnp
from jax import lax
from jax.experimental import pallas as pl
from jax.experimental.pallas import tpu as pltpu

F32 = jnp.float32
BF16 = jnp.bfloat16

POOL_WINDOWS = (2, 4, 8, 16)
HEAD_DIM = 128
N_KV_HEADS = 2
ROPE_AXIS_DIM = HEAD_DIM // 2
ROPE_THETA = 10000.0
GRID_W = 64
EPS = 1e-6

V7X_LANES = 128
V7X_SUBLANES = 8
V7X_VMEM_BYTES = 64 * 1024 * 1024

FFN_ROWS = 512
FFN_COL_CHUNK = 256
QKV_ROWS = 512
ATTN_Q_ROWS = 256
POOL_ROWS = 512
POOL_HALO = V7X_SUBLANES
VMEM_LIMIT = V7X_VMEM_BYTES - 8 * 1024 * 1024


def _rms(x, g):
    ms = jnp.mean(x * x, axis=-1, keepdims=True)
    return x * lax.rsqrt(ms + EPS) * g


def _const_spec(shape):
    return pl.BlockSpec(shape, lambda *_: (0,) * len(shape), pipeline_mode=pl.Buffered(1))


def _ffn_kernel(*refs, has_attn, d_ff):
    if has_attn:
        x_ref, o_ref, wo_ref, g_ref, wgu_ref, wd_ref, out_ref, h_scr, a_scr = refs
        out_ref[...] = x_ref[...] + jnp.dot(o_ref[...], wo_ref[...], preferred_element_type=F32)
        res_ref = out_ref
    else:
        x_ref, g_ref, wgu_ref, wd_ref, out_ref, h_scr, a_scr = refs
        res_ref = x_ref
    h_scr[...] = _rms(res_ref[...], g_ref[...]).astype(BF16)
    for c in range(d_ff // FFN_COL_CHUNK):
        lo = c * FFN_COL_CHUNK
        gate = jnp.dot(h_scr[...], wgu_ref[:, lo:lo + FFN_COL_CHUNK], preferred_element_type=F32)
        up = jnp.dot(h_scr[...], wgu_ref[:, d_ff + lo:d_ff + lo + FFN_COL_CHUNK], preferred_element_type=F32)
        a_scr[:, lo:lo + FFN_COL_CHUNK] = (jax.nn.silu(gate) * up).astype(BF16)
    out_ref[...] = res_ref[...] + jnp.dot(a_scr[...], wd_ref[...], preferred_element_type=F32)


def _ffn_layer(x2d, o2d, wo, g, wgu, wd):
    n, d = x2d.shape
    d_ff = wd.shape[0]
    has_attn = o2d is not None
    row_spec = pl.BlockSpec((FFN_ROWS, d), lambda i: (i, 0))
    in_specs = [row_spec]
    args = [x2d]
    if has_attn:
        in_specs += [row_spec, _const_spec(wo.shape)]
        args += [o2d, wo]
    in_specs += [_const_spec((1, d)), _const_spec(wgu.shape), _const_spec(wd.shape)]
    args += [g.reshape(1, d), wgu, wd]
    return pl.pallas_call(
        functools.partial(_ffn_kernel, has_attn=has_attn, d_ff=d_ff),
        grid=(n // FFN_ROWS,),
        in_specs=in_specs,
        out_specs=row_spec,
        out_shape=jax.ShapeDtypeStruct((n, d), F32),
        scratch_shapes=[pltpu.VMEM((FFN_ROWS, d), BF16), pltpu.VMEM((FFN_ROWS, d_ff), BF16)],
        compiler_params=pltpu.CompilerParams(
            dimension_semantics=("arbitrary",), vmem_limit_bytes=VMEM_LIMIT),
        name="ffn_attn" if has_attn else "ffn",
    )(*args)


def _pool_kernel(x_ref, xp_ref, xn_ref, g_ref, w_ref, b_ref, sc_ref, out_ref, hext, *, seq_len):
    i = pl.program_id(1)
    rows = x_ref.shape[0]
    gd = w_ref.shape[1]
    g = g_ref[...]
    hext[POOL_HALO:POOL_HALO + rows, :] = _rms(x_ref[...], g)
    hext[0:POOL_HALO, :] = jnp.where(i > 0, _rms(xp_ref[...], g), 0.0)
    hext[POOL_HALO + rows:, :] = jnp.where(i < pl.num_programs(1) - 1, _rms(xn_ref[...], g), 0.0)
    t = i * rows + lax.broadcasted_iota(jnp.int32, (rows, 1), 0)
    for gi, w in enumerate(POOL_WINDOWS):
        cs = slice(gi * gd, (gi + 1) * gd)
        half = w // 2
        wsum = hext[POOL_HALO - half:POOL_HALO - half + rows, cs]
        for off in range(-half + 1, half):
            wsum = wsum + hext[POOL_HALO + off:POOL_HALO + off + rows, cs]
        count = (jnp.minimum(t + half, seq_len) - jnp.maximum(t - half, 0)).astype(F32)
        pooled = wsum / count - hext[POOL_HALO:POOL_HALO + rows, cs]
        y = jnp.dot(pooled.astype(BF16), w_ref[gi], preferred_element_type=F32) + b_ref[:, cs]
        out_ref[:, cs] = x_ref[:, cs] + y * sc_ref[:, cs]


def _pool_layer(x, g, w, b, sc):
    bsz, t, d = x.shape
    nblk = POOL_ROWS // POOL_HALO
    last = t // POOL_HALO - 1
    tile = pl.BlockSpec((None, POOL_ROWS, d), lambda bi, i: (bi, i, 0))
    prev = pl.BlockSpec((None, POOL_HALO, d), lambda bi, i: (bi, jnp.maximum(i * nblk - 1, 0), 0))
    nxt = pl.BlockSpec((None, POOL_HALO, d), lambda bi, i: (bi, jnp.minimum((i + 1) * nblk, last), 0))
    return pl.pallas_call(
        functools.partial(_pool_kernel, seq_len=t),
        grid=(bsz, t // POOL_ROWS),
        in_specs=[tile, prev, nxt, _const_spec((1, d)), _const_spec(w.shape),
                  _const_spec((1, d)), _const_spec((1, d))],
        out_specs=tile,
        out_shape=jax.ShapeDtypeStruct(x.shape, F32),
        scratch_shapes=[pltpu.VMEM((POOL_ROWS + 2 * POOL_HALO, d), F32)],
        compiler_params=pltpu.CompilerParams(
            dimension_semantics=("arbitrary", "arbitrary"), vmem_limit_bytes=VMEM_LIMIT),
        name="pool",
    )(x, x, x, g.reshape(1, d), w, b.reshape(1, d), sc.reshape(1, d))


def _qkv_kernel(x_ref, g_ref, w_ref, gq_ref, gk_ref, cos_ref, sa_ref, sb_ref,
                q_ref, kt_ref, v_ref, *, n_heads):
    h = _rms(x_ref[...], g_ref[...]).astype(BF16)
    qkv = jnp.dot(h, w_ref[...], preferred_element_type=F32)
    cos, sa, sb = cos_ref[...], sa_ref[...], sb_ref[...]

    def norm_rope(xh, gain):
        y = _rms(xh, gain)
        return (y * cos + pltpu.roll(y, HEAD_DIM - ROPE_AXIS_DIM // 2, 1) * sa
                + pltpu.roll(y, ROPE_AXIS_DIM // 2, 1) * sb)

    for hd in range(n_heads):
        cs = slice(hd * HEAD_DIM, (hd + 1) * HEAD_DIM)
        q_ref[:, cs] = norm_rope(qkv[:, cs], gq_ref[...]).astype(BF16)
    for kh in range(N_KV_HEADS):
        k_lo = (n_heads + kh) * HEAD_DIM
        v_lo = (n_heads + N_KV_HEADS + kh) * HEAD_DIM
        k = norm_rope(qkv[:, k_lo:k_lo + HEAD_DIM], gk_ref[...])
        kt_ref[kh] = k.T.astype(BF16)
        v_ref[kh, :, :HEAD_DIM] = qkv[:, v_lo:v_lo + HEAD_DIM].astype(BF16)
        v_ref[kh, :, HEAD_DIM:] = jnp.ones((x_ref.shape[0], HEAD_DIM), BF16)


def _rope_tables(t):
    pos = jnp.arange(t, dtype=F32)
    row = jnp.floor(pos / GRID_W)
    col = pos - row * GRID_W
    inv_freq = ROPE_THETA ** (-jnp.arange(0, ROPE_AXIS_DIM, 2, dtype=F32) / ROPE_AXIS_DIM)
    ang_r = row[:, None] * inv_freq[None, :]
    ang_c = col[:, None] * inv_freq[None, :]
    ang = jnp.concatenate([ang_r, ang_r, ang_c, ang_c], axis=-1)
    first_half = (jnp.arange(HEAD_DIM) % ROPE_AXIS_DIM) < ROPE_AXIS_DIM // 2
    sin = jnp.sin(ang)
    sin_a = jnp.where(first_half[None, :], -sin, 0.0)
    sin_b = jnp.where(first_half[None, :], 0.0, sin)
    return jnp.cos(ang), sin_a, sin_b


def _qkv_layer(x, g, w, gq, gk, tables):
    bsz, t, d = x.shape
    n_heads = d // HEAD_DIM
    tab_spec = pl.BlockSpec((QKV_ROWS, HEAD_DIM), lambda bi, i: (i, 0))
    return pl.pallas_call(
        functools.partial(_qkv_kernel, n_heads=n_heads),
        grid=(bsz, t // QKV_ROWS),
        in_specs=[pl.BlockSpec((None, QKV_ROWS, d), lambda bi, i: (bi, i, 0)),
                  _const_spec((1, d)), _const_spec(w.shape),
                  _const_spec((1, HEAD_DIM)), _const_spec((1, HEAD_DIM)),
                  tab_spec, tab_spec, tab_spec],
        out_specs=[pl.BlockSpec((None, QKV_ROWS, d), lambda bi, i: (bi, i, 0)),
                   pl.BlockSpec((None, N_KV_HEADS, HEAD_DIM, QKV_ROWS), lambda bi, i: (bi, 0, 0, i)),
                   pl.BlockSpec((None, N_KV_HEADS, QKV_ROWS, 2 * HEAD_DIM), lambda bi, i: (bi, 0, i, 0))],
        out_shape=[jax.ShapeDtypeStruct((bsz, t, d), BF16),
                   jax.ShapeDtypeStruct((bsz, N_KV_HEADS, HEAD_DIM, t), BF16),
                   jax.ShapeDtypeStruct((bsz, N_KV_HEADS, t, 2 * HEAD_DIM), BF16)],
        compiler_params=pltpu.CompilerParams(
            dimension_semantics=("arbitrary", "arbitrary"), vmem_limit_bytes=VMEM_LIMIT),
        name="qkv",
    )(x, g.reshape(1, d), w, gq.reshape(1, HEAD_DIM), gk.reshape(1, HEAD_DIM), *tables)


def _attn_kernel(q_ref, kt_ref, v_ref, o_ref, *, group):
    for hd in range(group):
        cs = slice(hd * HEAD_DIM, (hd + 1) * HEAD_DIM)
        s = jnp.dot(q_ref[:, cs], kt_ref[...], preferred_element_type=F32)
        m = jnp.max(s, axis=-1, keepdims=True)
        p = jnp.exp(s - m).astype(BF16)
        ov = jnp.dot(p, v_ref[...], preferred_element_type=F32)
        o_ref[:, cs] = (ov[:, :HEAD_DIM] / ov[:, HEAD_DIM:]).astype(BF16)


def _attn_layer(q, kt, v):
    bsz, t, d = q.shape
    group = d // HEAD_DIM // N_KV_HEADS
    gw = group * HEAD_DIM
    q_spec = pl.BlockSpec((None, ATTN_Q_ROWS, gw), lambda bi, kh, i: (bi, i, kh))
    return pl.pallas_call(
        functools.partial(_attn_kernel, group=group),
        grid=(bsz, N_KV_HEADS, t // ATTN_Q_ROWS),
        in_specs=[q_spec,
                  pl.BlockSpec((None, None, HEAD_DIM, t), lambda bi, kh, i: (bi, kh, 0, 0)),
                  pl.BlockSpec((None, None, t, 2 * HEAD_DIM), lambda bi, kh, i: (bi, kh, 0, 0))],
        out_specs=q_spec,
        out_shape=jax.ShapeDtypeStruct((bsz, t, d), BF16),
        compiler_params=pltpu.CompilerParams(
            dimension_semantics=("arbitrary", "arbitrary", "arbitrary"), vmem_limit_bytes=VMEM_LIMIT),
        name="attn",
    )(q, kt, v)


def _trunk(x, p):
    bsz, t, d = x.shape
    depth = p["norm_mix"].shape[0]
    tables = _rope_tables(t)
    for i in range(depth):
        j = i // 2
        if i % 2 == 0:
            x = _pool_layer(x, p["norm_mix"][i], p["pool_w"][j], p["pool_b"][j], p["pool_scale"][j])
            o2d, wo = None, None
        else:
            q, kt, v = _qkv_layer(x, p["norm_mix"][i], p["attn_w_qkv"][j],
                                  p["attn_q_gain"][j], p["attn_k_gain"][j], tables)
            o2d = _attn_layer(q, kt, v).reshape(bsz * t, d)
            wo = p["attn_w_o"][j]
        x = _ffn_layer(x.reshape(bsz * t, d), o2d, wo, p["norm_ffn"][i],
                       p["ffn_w_gate_up"][i], p["ffn_w_down"][i]).reshape(bsz, t, d)
    return x


def kernel(x_prompt, x_sample, norm_mix, norm_ffn, pool_w, pool_b, pool_scale, attn_w_qkv, attn_q_gain, attn_k_gain, attn_w_o, ffn_w_gate_up, ffn_w_down):
    p = dict(
        norm_mix=norm_mix, norm_ffn=norm_ffn,
        pool_w=pool_w.astype(BF16), pool_b=pool_b, pool_scale=pool_scale,
        attn_w_qkv=attn_w_qkv.astype(BF16),
        attn_q_gain=attn_q_gain * (1.0 / math.sqrt(HEAD_DIM)), attn_k_gain=attn_k_gain,
        attn_w_o=attn_w_o.astype(BF16),
        ffn_w_gate_up=ffn_w_gate_up.astype(BF16), ffn_w_down=ffn_w_down.astype(BF16),
    )
    return (_trunk(x_prompt, p), _trunk(x_sample, p))
```

```python
import functools
import math

import jax
import jax.numpy as jnp
from jax import lax
from jax.experimental import pallas as pl
from jax.experimental.pallas import tpu as pltpu

F32 = jnp.float32
BF16 = jnp.bfloat16

POOL_WINDOWS = (2, 4, 8, 16)
HEAD_DIM = 128
N_KV_HEADS = 2
ROPE_AXIS_DIM = HEAD_DIM // 2
ROPE_THETA = 10000.0
GRID_W = 64
EPS = 1e-6

V7X_LANES = 128
V7X_SUBLANES = 8
V7X_VMEM_BYTES = 64 * 1024 * 1024

FFN_ROWS = 512
FFN_COL_CHUNK = 256
QKV_ROWS = 512
ATTN_Q_ROWS = 512
ATTN_KEY_CHUNK = 512
ATTN_STAGES = 3
VT_ROWS = HEAD_DIM + 16
POOL_ROWS = 512
POOL_HALO = V7X_SUBLANES
VMEM_LIMIT = V7X_VMEM_BYTES - 8 * 1024 * 1024


def _rms(x, g):
    ms = jnp.mean(x * x, axis=-1, keepdims=True)
    return x * lax.rsqrt(ms + EPS) * g


def _const_spec(shape):
    return pl.BlockSpec(shape, lambda *_: (0,) * len(shape), pipeline_mode=pl.Buffered(1))


def _ffn_kernel(*refs, has_attn, d_ff):
    if has_attn:
        x_ref, o_ref, wo_ref, g_ref, wgu_ref, wd_ref, out_ref, h_scr, a_scr = refs
        out_ref[...] = x_ref[...] + jnp.dot(o_ref[...], wo_ref[...], preferred_element_type=F32)
        res_ref = out_ref
    else:
        x_ref, g_ref, wgu_ref, wd_ref, out_ref, h_scr, a_scr = refs
        res_ref = x_ref
    h_scr[...] = _rms(res_ref[...], g_ref[...]).astype(BF16)
    for c in range(d_ff // FFN_COL_CHUNK):
        lo = c * FFN_COL_CHUNK
        gate = jnp.dot(h_scr[...], wgu_ref[:, lo:lo + FFN_COL_CHUNK], preferred_element_type=F32)
        up = jnp.dot(h_scr[...], wgu_ref[:, d_ff + lo:d_ff + lo + FFN_COL_CHUNK], preferred_element_type=F32)
        a_scr[:, lo:lo + FFN_COL_CHUNK] = (jax.nn.silu(gate) * up).astype(BF16)
    out_ref[...] = res_ref[...] + jnp.dot(a_scr[...], wd_ref[...], preferred_element_type=F32)


def _ffn_layer(x2d, o2d, wo, g, wgu, wd):
    n, d = x2d.shape
    d_ff = wd.shape[0]
    has_attn = o2d is not None
    row_spec = pl.BlockSpec((FFN_ROWS, d), lambda i: (i, 0))
    in_specs = [row_spec]
    args = [x2d]
    if has_attn:
        in_specs += [row_spec, _const_spec(wo.shape)]
        args += [o2d, wo]
    in_specs += [_const_spec((1, d)), _const_spec(wgu.shape), _const_spec(wd.shape)]
    args += [g.reshape(1, d), wgu, wd]
    return pl.pallas_call(
        functools.partial(_ffn_kernel, has_attn=has_attn, d_ff=d_ff),
        grid=(n // FFN_ROWS,),
        in_specs=in_specs,
        out_specs=row_spec,
        out_shape=jax.ShapeDtypeStruct((n, d), F32),
        scratch_shapes=[pltpu.VMEM((FFN_ROWS, d), BF16), pltpu.VMEM((FFN_ROWS, d_ff), BF16)],
        compiler_params=pltpu.CompilerParams(
            dimension_semantics=("arbitrary",), vmem_limit_bytes=VMEM_LIMIT),
        name="ffn_attn" if has_attn else "ffn",
    )(*args)


def _pool_kernel(x_ref, xp_ref, xn_ref, g_ref, w_ref, b_ref, sc_ref, out_ref, hext, *, seq_len):
    i = pl.program_id(1)
    rows = x_ref.shape[0]
    gd = w_ref.shape[1]
    g = g_ref[...]
    hext[POOL_HALO:POOL_HALO + rows, :] = _rms(x_ref[...], g)
    hext[0:POOL_HALO, :] = jnp.where(i > 0, _rms(xp_ref[...], g), 0.0)
    hext[POOL_HALO + rows:, :] = jnp.where(i < pl.num_programs(1) - 1, _rms(xn_ref[...], g), 0.0)
    t = i * rows + lax.broadcasted_iota(jnp.int32, (rows, 1), 0)
    for gi, w in enumerate(POOL_WINDOWS):
        cs = slice(gi * gd, (gi + 1) * gd)
        half = w // 2
        wsum = hext[POOL_HALO - half:POOL_HALO - half + rows, cs]
        for off in range(-half + 1, half):
            wsum = wsum + hext[POOL_HALO + off:POOL_HALO + off + rows, cs]
        count = (jnp.minimum(t + half, seq_len) - jnp.maximum(t - half, 0)).astype(F32)
        pooled = wsum / count - hext[POOL_HALO:POOL_HALO + rows, cs]
        y = jnp.dot(pooled.astype(BF16), w_ref[gi], preferred_element_type=F32) + b_ref[:, cs]
        out_ref[:, cs] = x_ref[:, cs] + y * sc_ref[:, cs]


def _pool_layer(x, g, w, b, sc):
    bsz, t, d = x.shape
    nblk = POOL_ROWS // POOL_HALO
    last = t // POOL_HALO - 1
    tile = pl.BlockSpec((None, POOL_ROWS, d), lambda bi, i: (bi, i, 0))
    prev = pl.BlockSpec((None, POOL_HALO, d), lambda bi, i: (bi, jnp.maximum(i * nblk - 1, 0), 0))
    nxt = pl.BlockSpec((None, POOL_HALO, d), lambda bi, i: (bi, jnp.minimum((i + 1) * nblk, last), 0))
    return pl.pallas_call(
        functools.partial(_pool_kernel, seq_len=t),
        grid=(bsz, t // POOL_ROWS),
        in_specs=[tile, prev, nxt, _const_spec((1, d)), _const_spec(w.shape),
                  _const_spec((1, d)), _const_spec((1, d))],
        out_specs=tile,
        out_shape=jax.ShapeDtypeStruct(x.shape, F32),
        scratch_shapes=[pltpu.VMEM((POOL_ROWS + 2 * POOL_HALO, d), F32)],
        compiler_params=pltpu.CompilerParams(
            dimension_semantics=("arbitrary", "arbitrary"), vmem_limit_bytes=VMEM_LIMIT),
        name="pool",
    )(x, x, x, g.reshape(1, d), w, b.reshape(1, d), sc.reshape(1, d))


def _qkv_kernel(x_ref, g_ref, w_ref, gq_ref, gk_ref, cos_ref, sa_ref, sb_ref,
                q_ref, k_ref, vt_ref, *, n_heads):
    h = _rms(x_ref[...], g_ref[...]).astype(BF16)
    qkv = jnp.dot(h, w_ref[...], preferred_element_type=F32)
    cos, sa, sb = cos_ref[...], sa_ref[...], sb_ref[...]

    def norm_rope(xh, gain):
        y = _rms(xh, gain)
        return (y * cos + pltpu.roll(y, HEAD_DIM - ROPE_AXIS_DIM // 2, 1) * sa
                + pltpu.roll(y, ROPE_AXIS_DIM // 2, 1) * sb)

    for hd in range(n_heads):
        cs = slice(hd * HEAD_DIM, (hd + 1) * HEAD_DIM)
        q_ref[:, cs] = norm_rope(qkv[:, cs], gq_ref[...]).astype(BF16)
    for kh in range(N_KV_HEADS):
        k_lo = (n_heads + kh) * HEAD_DIM
        v_lo = (n_heads + N_KV_HEADS + kh) * HEAD_DIM
        k_ref[kh] = norm_rope(qkv[:, k_lo:k_lo + HEAD_DIM], gk_ref[...]).astype(BF16)
        vt_ref[kh, :HEAD_DIM, :] = qkv[:, v_lo:v_lo + HEAD_DIM].T.astype(BF16)
        vt_ref[kh, HEAD_DIM:, :] = jnp.ones((VT_ROWS - HEAD_DIM, x_ref.shape[0]), BF16)


def _rope_tables(t):
    pos = jnp.arange(t, dtype=F32)
    row = jnp.floor(pos / GRID_W)
    col = pos - row * GRID_W
    inv_freq = ROPE_THETA ** (-jnp.arange(0, ROPE_AXIS_DIM, 2, dtype=F32) / ROPE_AXIS_DIM)
    ang_r = row[:, None] * inv_freq[None, :]
    ang_c = col[:, None] * inv_freq[None, :]
    ang = jnp.concatenate([ang_r, ang_r, ang_c, ang_c], axis=-1)
    first_half = (jnp.arange(HEAD_DIM) % ROPE_AXIS_DIM) < ROPE_AXIS_DIM // 2
    sin = jnp.sin(ang)
    sin_a = jnp.where(first_half[None, :], -sin, 0.0)
    sin_b = jnp.where(first_half[None, :], 0.0, sin)
    return jnp.cos(ang), sin_a, sin_b


def _qkv_layer(x, g, w, gq, gk, tables):
    bsz, t, d = x.shape
    n_heads = d // HEAD_DIM
    tab_spec = pl.BlockSpec((QKV_ROWS, HEAD_DIM), lambda bi, i: (i, 0))
    return pl.pallas_call(
        functools.partial(_qkv_kernel, n_heads=n_heads),
        grid=(bsz, t // QKV_ROWS),
        in_specs=[pl.BlockSpec((None, QKV_ROWS, d), lambda bi, i: (bi, i, 0)),
                  _const_spec((1, d)), _const_spec(w.shape),
                  _const_spec((1, HEAD_DIM)), _const_spec((1, HEAD_DIM)),
                  tab_spec, tab_spec, tab_spec],
        out_specs=[pl.BlockSpec((None, QKV_ROWS, d), lambda bi, i: (bi, i, 0)),
                   pl.BlockSpec((None, N_KV_HEADS, QKV_ROWS, HEAD_DIM), lambda bi, i: (bi, 0, i, 0)),
                   pl.BlockSpec((None, N_KV_HEADS, VT_ROWS, QKV_ROWS), lambda bi, i: (bi, 0, 0, i))],
        out_shape=[jax.ShapeDtypeStruct((bsz, t, d), BF16),
                   jax.ShapeDtypeStruct((bsz, N_KV_HEADS, t, HEAD_DIM), BF16),
                   jax.ShapeDtypeStruct((bsz, N_KV_HEADS, VT_ROWS, t), BF16)],
        compiler_params=pltpu.CompilerParams(
            dimension_semantics=("arbitrary", "arbitrary"), vmem_limit_bytes=VMEM_LIMIT),
        name="qkv",
    )(x, g.reshape(1, d), w, gq.reshape(1, HEAD_DIM), gk.reshape(1, HEAD_DIM), *tables)


def _attn_kernel(q_ref, k_ref, vt_ref, o_ref, st_scr, *, group):
    rows = q_ref.shape[0]
    n_chunks = k_ref.shape[0] // ATTN_KEY_CHUNK
    units = [(hd, c) for hd in range(group) for c in range(n_chunks)]
    n_units = len(units)
    cmax = {}

    def scores(u):
        hd, c = units[u]
        st_scr[u % ATTN_STAGES] = lax.dot_general(
            k_ref[c * ATTN_KEY_CHUNK:(c + 1) * ATTN_KEY_CHUNK, :],
            q_ref[:, hd * HEAD_DIM:(hd + 1) * HEAD_DIM],
            (((1,), (1,)), ((), ())), preferred_element_type=F32)

    def chunk_max(u):
        cmax[u] = jnp.max(st_scr[u % ATTN_STAGES], axis=0, keepdims=True)

    scores(0)
    scores(1)
    chunk_max(0)
    for u, (hd, c) in enumerate(units):
        if u + 2 < n_units:
            scores(u + 2)
        if u + 1 < n_units:
            chunk_max(u + 1)
        if c == 0:
            m = jnp.full((1, rows), -jnp.inf, F32)
            acc = jnp.zeros((VT_ROWS, rows), F32)
        m_new = jnp.maximum(m, cmax.pop(u))
        alpha = jnp.exp2(m - m_new)
        e = jnp.exp2(st_scr[u % ATTN_STAGES] - m_new)
        acc = alpha * acc + jnp.dot(vt_ref[:, c * ATTN_KEY_CHUNK:(c + 1) * ATTN_KEY_CHUNK],
                                    e.astype(BF16), preferred_element_type=F32)
        m = m_new
        if c == n_chunks - 1:
            out = acc[:HEAD_DIM] / acc[HEAD_DIM:HEAD_DIM + 1]
            o_ref[:, hd * HEAD_DIM:(hd + 1) * HEAD_DIM] = out.T.astype(BF16)


def _attn_layer(q, k, vt):
    bsz, t, d = q.shape
    group = d // HEAD_DIM // N_KV_HEADS
    gw = group * HEAD_DIM
    q_spec = pl.BlockSpec((None, ATTN_Q_ROWS, gw), lambda bi, kh, i: (bi, i, kh))
    return pl.pallas_call(
        functools.partial(_attn_kernel, group=group),
        grid=(bsz, N_KV_HEADS, t // ATTN_Q_ROWS),
        in_specs=[q_spec,
                  pl.BlockSpec((None, None, t, HEAD_DIM), lambda bi, kh, i: (bi, kh, 0, 0)),
                  pl.BlockSpec((None, None, VT_ROWS, t), lambda bi, kh, i: (bi, kh, 0, 0))],
        out_specs=q_spec,
        out_shape=jax.ShapeDtypeStruct((bsz, t, d), BF16),
        scratch_shapes=[pltpu.VMEM((ATTN_STAGES, ATTN_KEY_CHUNK, ATTN_Q_ROWS), F32)],
        compiler_params=pltpu.CompilerParams(
            dimension_semantics=("arbitrary", "arbitrary", "arbitrary"), vmem_limit_bytes=VMEM_LIMIT),
        name="attn",
    )(q, k, vt)


def _trunk(x, p):
    bsz, t, d = x.shape
    depth = p["norm_mix"].shape[0]
    tables = _rope_tables(t)
    for i in range(depth):
        j = i // 2
        if i % 2 == 0:
            x = _pool_layer(x, p["norm_mix"][i], p["pool_w"][j], p["pool_b"][j], p["pool_scale"][j])
            o2d, wo = None, None
        else:
            q, k, vt = _qkv_layer(x, p["norm_mix"][i], p["attn_w_qkv"][j],
                                  p["attn_q_gain"][j], p["attn_k_gain"][j], tables)
            o2d = _attn_layer(q, k, vt).reshape(bsz * t, d)
            wo = p["attn_w_o"][j]
        x = _ffn_layer(x.reshape(bsz * t, d), o2d, wo, p["norm_ffn"][i],
                       p["ffn_w_gate_up"][i], p["ffn_w_down"][i]).reshape(bsz, t, d)
    return x


def kernel(x_prompt, x_sample, norm_mix, norm_ffn, pool_w, pool_b, pool_scale, attn_w_qkv, attn_q_gain, attn_k_gain, attn_w_o, ffn_w_gate_up, ffn_w_down):
    p = dict(
        norm_mix=norm_mix, norm_ffn=norm_ffn,
        pool_w=pool_w.astype(BF16), pool_b=pool_b, pool_scale=pool_scale,
        attn_w_qkv=attn_w_qkv.astype(BF16),
        attn_q_gain=attn_q_gain * (math.log2(math.e) / math.sqrt(HEAD_DIM)), attn_k_gain=attn_k_gain,
        attn_w_o=attn_w_o.astype(BF16),
        ffn_w_gate_up=ffn_w_gate_up.astype(BF16), ffn_w_down=ffn_w_down.astype(BF16),
    )
    return (_trunk(x_prompt, p), _trunk(x_sample, p))
```

```python
import functools
import math

import jax
import jax.numpy as jnp
from jax import lax
from jax.experimental import pallas as pl
from jax.experimental.pallas import tpu as pltpu

F32 = jnp.float32
BF16 = jnp.bfloat16

POOL_WINDOWS = (2, 4, 8, 16)
HEAD_DIM = 128
N_KV_HEADS = 2
ROPE_AXIS_DIM = HEAD_DIM // 2
ROPE_THETA = 10000.0
GRID_W = 64
EPS = 1e-6

V7X_LANES = 128
V7X_SUBLANES = 8
V7X_VMEM_BYTES = 64 * 1024 * 1024

FFN_ROWS = 512
FFN_COL_CHUNK = 256
QKV_ROWS = 512
ATTN_Q_ROWS = 512
ATTN_KEY_CHUNK = 512
ATTN_EXP2_HEADROOM = 60.0
ATTN_FIXED_OFFSET_MAX_BOUND = 75.0
BF16_ROUNDING_MARGIN = 1.02
ATTN_STAGES = 3
VT_ROWS = HEAD_DIM + 16
POOL_ROWS = 512
POOL_HALO = V7X_SUBLANES
VMEM_LIMIT = V7X_VMEM_BYTES - 8 * 1024 * 1024


def _rms(x, g):
    ms = jnp.mean(x * x, axis=-1, keepdims=True)
    return x * lax.rsqrt(ms + EPS) * g


def _const_spec(shape):
    return pl.BlockSpec(shape, lambda *_: (0,) * len(shape), pipeline_mode=pl.Buffered(1))


def _ffn_kernel(*refs, has_attn, d_ff):
    if has_attn:
        x_ref, o_ref, wo_ref, g_ref, wgu_ref, wd_ref, out_ref, h_scr, a_scr = refs
        out_ref[...] = x_ref[...] + jnp.dot(o_ref[...], wo_ref[...], preferred_element_type=F32)
        res_ref = out_ref
    else:
        x_ref, g_ref, wgu_ref, wd_ref, out_ref, h_scr, a_scr = refs
        res_ref = x_ref
    h_scr[...] = _rms(res_ref[...], g_ref[...]).astype(BF16)
    for c in range(d_ff // FFN_COL_CHUNK):
        lo = c * FFN_COL_CHUNK
        gate = jnp.dot(h_scr[...], wgu_ref[:, lo:lo + FFN_COL_CHUNK], preferred_element_type=F32)
        up = jnp.dot(h_scr[...], wgu_ref[:, d_ff + lo:d_ff + lo + FFN_COL_CHUNK], preferred_element_type=F32)
        a_scr[:, lo:lo + FFN_COL_CHUNK] = (jax.nn.silu(gate) * up).astype(BF16)
    out_ref[...] = res_ref[...] + jnp.dot(a_scr[...], wd_ref[...], preferred_element_type=F32)


def _ffn_layer(x2d, o2d, wo, g, wgu, wd):
    n, d = x2d.shape
    d_ff = wd.shape[0]
    has_attn = o2d is not None
    row_spec = pl.BlockSpec((FFN_ROWS, d), lambda i: (i, 0))
    in_specs = [row_spec]
    args = [x2d]
    if has_attn:
        in_specs += [row_spec, _const_spec(wo.shape)]
        args += [o2d, wo]
    in_specs += [_const_spec((1, d)), _const_spec(wgu.shape), _const_spec(wd.shape)]
    args += [g.reshape(1, d), wgu, wd]
    return pl.pallas_call(
        functools.partial(_ffn_kernel, has_attn=has_attn, d_ff=d_ff),
        grid=(n // FFN_ROWS,),
        in_specs=in_specs,
        out_specs=row_spec,
        out_shape=jax.ShapeDtypeStruct((n, d), F32),
        scratch_shapes=[pltpu.VMEM((FFN_ROWS, d), BF16), pltpu.VMEM((FFN_ROWS, d_ff), BF16)],
        compiler_params=pltpu.CompilerParams(
            dimension_semantics=("arbitrary",), vmem_limit_bytes=VMEM_LIMIT),
        name="ffn_attn" if has_attn else "ffn",
    )(*args)


def _pool_kernel(x_ref, xp_ref, xn_ref, g_ref, w_ref, b_ref, sc_ref, out_ref, hext, *, seq_len):
    i = pl.program_id(1)
    rows = x_ref.shape[0]
    gd = w_ref.shape[1]
    g = g_ref[...]
    hext[POOL_HALO:POOL_HALO + rows, :] = _rms(x_ref[...], g)
    hext[0:POOL_HALO, :] = jnp.where(i > 0, _rms(xp_ref[...], g), 0.0)
    hext[POOL_HALO + rows:, :] = jnp.where(i < pl.num_programs(1) - 1, _rms(xn_ref[...], g), 0.0)
    t = i * rows + lax.broadcasted_iota(jnp.int32, (rows, 1), 0)
    for gi, w in enumerate(POOL_WINDOWS):
        cs = slice(gi * gd, (gi + 1) * gd)
        half = w // 2
        wsum = hext[POOL_HALO - half:POOL_HALO - half + rows, cs]
        for off in range(-half + 1, half):
            wsum = wsum + hext[POOL_HALO + off:POOL_HALO + off + rows, cs]
        count = (jnp.minimum(t + half, seq_len) - jnp.maximum(t - half, 0)).astype(F32)
        pooled = wsum / count - hext[POOL_HALO:POOL_HALO + rows, cs]
        y = jnp.dot(pooled.astype(BF16), w_ref[gi], preferred_element_type=F32) + b_ref[:, cs]
        out_ref[:, cs] = x_ref[:, cs] + y * sc_ref[:, cs]


def _pool_layer(x, g, w, b, sc):
    bsz, t, d = x.shape
    nblk = POOL_ROWS // POOL_HALO
    last = t // POOL_HALO - 1
    tile = pl.BlockSpec((None, POOL_ROWS, d), lambda bi, i: (bi, i, 0))
    prev = pl.BlockSpec((None, POOL_HALO, d), lambda bi, i: (bi, jnp.maximum(i * nblk - 1, 0), 0))
    nxt = pl.BlockSpec((None, POOL_HALO, d), lambda bi, i: (bi, jnp.minimum((i + 1) * nblk, last), 0))
    return pl.pallas_call(
        functools.partial(_pool_kernel, seq_len=t),
        grid=(bsz, t // POOL_ROWS),
        in_specs=[tile, prev, nxt, _const_spec((1, d)), _const_spec(w.shape),
                  _const_spec((1, d)), _const_spec((1, d))],
        out_specs=tile,
        out_shape=jax.ShapeDtypeStruct(x.shape, F32),
        scratch_shapes=[pltpu.VMEM((POOL_ROWS + 2 * POOL_HALO, d), F32)],
        compiler_params=pltpu.CompilerParams(
            dimension_semantics=("arbitrary", "arbitrary"), vmem_limit_bytes=VMEM_LIMIT),
        name="pool",
    )(x, x, x, g.reshape(1, d), w, b.reshape(1, d), sc.reshape(1, d))


def _qkv_kernel(x_ref, g_ref, w_ref, gq_ref, gk_ref, cos_ref, sa_ref, sb_ref,
                q_ref, k_ref, vt_ref, *, n_heads):
    h = _rms(x_ref[...], g_ref[...]).astype(BF16)
    qkv = jnp.dot(h, w_ref[...], preferred_element_type=F32)
    cos, sa, sb = cos_ref[...], sa_ref[...], sb_ref[...]

    def norm_rope(xh, gain):
        y = _rms(xh, gain)
        return (y * cos + pltpu.roll(y, HEAD_DIM - ROPE_AXIS_DIM // 2, 1) * sa
                + pltpu.roll(y, ROPE_AXIS_DIM // 2, 1) * sb)

    for hd in range(n_heads):
        cs = slice(hd * HEAD_DIM, (hd + 1) * HEAD_DIM)
        q_ref[:, cs] = norm_rope(qkv[:, cs], gq_ref[...]).astype(BF16)
    for kh in range(N_KV_HEADS):
        k_lo = (n_heads + kh) * HEAD_DIM
        v_lo = (n_heads + N_KV_HEADS + kh) * HEAD_DIM
        k_ref[kh] = norm_rope(qkv[:, k_lo:k_lo + HEAD_DIM], gk_ref[...]).astype(BF16)
        vt_ref[kh, :HEAD_DIM, :] = qkv[:, v_lo:v_lo + HEAD_DIM].T.astype(BF16)
        vt_ref[kh, HEAD_DIM:, :] = jnp.ones((VT_ROWS - HEAD_DIM, x_ref.shape[0]), BF16)


def _rope_tables(t):
    pos = jnp.arange(t, dtype=F32)
    row = jnp.floor(pos / GRID_W)
    col = pos - row * GRID_W
    inv_freq = ROPE_THETA ** (-jnp.arange(0, ROPE_AXIS_DIM, 2, dtype=F32) / ROPE_AXIS_DIM)
    ang_r = row[:, None] * inv_freq[None, :]
    ang_c = col[:, None] * inv_freq[None, :]
    ang = jnp.concatenate([ang_r, ang_r, ang_c, ang_c], axis=-1)
    first_half = (jnp.arange(HEAD_DIM) % ROPE_AXIS_DIM) < ROPE_AXIS_DIM // 2
    sin = jnp.sin(ang)
    sin_a = jnp.where(first_half[None, :], -sin, 0.0)
    sin_b = jnp.where(first_half[None, :], 0.0, sin)
    return jnp.cos(ang), sin_a, sin_b


def _qkv_layer(x, g, w, gq, gk, tables):
    bsz, t, d = x.shape
    n_heads = d // HEAD_DIM
    tab_spec = pl.BlockSpec((QKV_ROWS, HEAD_DIM), lambda bi, i: (i, 0))
    return pl.pallas_call(
        functools.partial(_qkv_kernel, n_heads=n_heads),
        grid=(bsz, t // QKV_ROWS),
        in_specs=[pl.BlockSpec((None, QKV_ROWS, d), lambda bi, i: (bi, i, 0)),
                  _const_spec((1, d)), _const_spec(w.shape),
                  _const_spec((1, HEAD_DIM)), _const_spec((1, HEAD_DIM)),
                  tab_spec, tab_spec, tab_spec],
        out_specs=[pl.BlockSpec((None, QKV_ROWS, d), lambda bi, i: (bi, i, 0)),
                   pl.BlockSpec((None, N_KV_HEADS, QKV_ROWS, HEAD_DIM), lambda bi, i: (bi, 0, i, 0)),
                   pl.BlockSpec((None, N_KV_HEADS, VT_ROWS, QKV_ROWS), lambda bi, i: (bi, 0, 0, i))],
        out_shape=[jax.ShapeDtypeStruct((bsz, t, d), BF16),
                   jax.ShapeDtypeStruct((bsz, N_KV_HEADS, t, HEAD_DIM), BF16),
                   jax.ShapeDtypeStruct((bsz, N_KV_HEADS, VT_ROWS, t), BF16)],
        compiler_params=pltpu.CompilerParams(
            dimension_semantics=("arbitrary", "arbitrary"), vmem_limit_bytes=VMEM_LIMIT),
        name="qkv",
    )(x, g.reshape(1, d), w, gq.reshape(1, HEAD_DIM), gk.reshape(1, HEAD_DIM), *tables)


def _attn_units(k_ref, group):
    n_chunks = k_ref.shape[0] // ATTN_KEY_CHUNK
    return [(hd, c) for hd in range(group) for c in range(n_chunks)], n_chunks


def _attn_scores(q_ref, k_ref, hd, c):
    return lax.dot_general(
        k_ref[c * ATTN_KEY_CHUNK:(c + 1) * ATTN_KEY_CHUNK, :],
        q_ref[:, hd * HEAD_DIM:(hd + 1) * HEAD_DIM],
        (((1,), (1,)), ((), ())), preferred_element_type=F32)


def _attn_store(o_ref, hd, acc):
    out = acc[:HEAD_DIM] / acc[HEAD_DIM:HEAD_DIM + 1]
    o_ref[:, hd * HEAD_DIM:(hd + 1) * HEAD_DIM] = out.T.astype(BF16)


def _attn_fixed_offset(offset, q_ref, k_ref, vt_ref, o_ref, group):
    units, n_chunks = _attn_units(k_ref, group)

    def probs(u):
        hd, c = units[u]
        return jnp.exp2(_attn_scores(q_ref, k_ref, hd, c) - offset).astype(BF16)

    p_next = probs(0)
    for u, (hd, c) in enumerate(units):
        p = p_next
        if u + 1 < len(units):
            p_next = probs(u + 1)
        pv = jnp.dot(vt_ref[:, c * ATTN_KEY_CHUNK:(c + 1) * ATTN_KEY_CHUNK], p,
                     preferred_element_type=F32)
        acc = pv if c == 0 else acc + pv
        if c == n_chunks - 1:
            _attn_store(o_ref, hd, acc)


def _attn_online(q_ref, k_ref, vt_ref, o_ref, st_scr, group):
    rows = q_ref.shape[0]
    units, n_chunks = _attn_units(k_ref, group)
    n_units = len(units)
    cmax = {}

    def scores(u):
        st_scr[u % ATTN_STAGES] = _attn_scores(q_ref, k_ref, *units[u])

    def chunk_max(u):
        cmax[u] = jnp.max(st_scr[u % ATTN_STAGES], axis=0, keepdims=True)

    scores(0)
    scores(1)
    chunk_max(0)
    for u, (hd, c) in enumerate(units):
        if u + 2 < n_units:
            scores(u + 2)
        if u + 1 < n_units:
            chunk_max(u + 1)
        if c == 0:
            m = jnp.full((1, rows), -jnp.inf, F32)
            acc = jnp.zeros((VT_ROWS, rows), F32)
        m_new = jnp.maximum(m, cmax.pop(u))
        alpha = jnp.exp2(m - m_new)
        e = jnp.exp2(st_scr[u % ATTN_STAGES] - m_new)
        acc = alpha * acc + jnp.dot(vt_ref[:, c * ATTN_KEY_CHUNK:(c + 1) * ATTN_KEY_CHUNK],
                                    e.astype(BF16), preferred_element_type=F32)
        m = m_new
        if c == n_chunks - 1:
            _attn_store(o_ref, hd, acc)


def _attn_kernel(bound_ref, q_ref, k_ref, vt_ref, o_ref, st_scr, *, group):
    bound = bound_ref[0]

    @pl.when(bound <= ATTN_FIXED_OFFSET_MAX_BOUND)
    def _():
        _attn_fixed_offset(bound - ATTN_EXP2_HEADROOM, q_ref, k_ref, vt_ref, o_ref, group)

    @pl.when(jnp.logical_not(bound <= ATTN_FIXED_OFFSET_MAX_BOUND))
    def _():
        _attn_online(q_ref, k_ref, vt_ref, o_ref, st_scr, group)


def _score_bound(gq, gk):
    return (HEAD_DIM * BF16_ROUNDING_MARGIN * jnp.max(jnp.abs(gq)) * jnp.max(jnp.abs(gk))).reshape(1)


def _attn_layer(q, k, vt, bound):
    bsz, t, d = q.shape
    group = d // HEAD_DIM // N_KV_HEADS
    gw = group * HEAD_DIM
    q_spec = pl.BlockSpec((None, ATTN_Q_ROWS, gw), lambda bi, kh, i: (bi, i, kh))
    return pl.pallas_call(
        functools.partial(_attn_kernel, group=group),
        grid=(bsz, N_KV_HEADS, t // ATTN_Q_ROWS),
        in_specs=[pl.BlockSpec(memory_space=pltpu.SMEM),
                  q_spec,
                  pl.BlockSpec((None, None, t, HEAD_DIM), lambda bi, kh, i: (bi, kh, 0, 0)),
                  pl.BlockSpec((None, None, VT_ROWS, t), lambda bi, kh, i: (bi, kh, 0, 0))],
        out_specs=q_spec,
        out_shape=jax.ShapeDtypeStruct((bsz, t, d), BF16),
        scratch_shapes=[pltpu.VMEM((ATTN_STAGES, ATTN_KEY_CHUNK, ATTN_Q_ROWS), F32)],
        compiler_params=pltpu.CompilerParams(
            dimension_semantics=("arbitrary", "arbitrary", "arbitrary"), vmem_limit_bytes=VMEM_LIMIT),
        name="attn",
    )(bound, q, k, vt)


def _trunk(x, p):
    bsz, t, d = x.shape
    depth = p["norm_mix"].shape[0]
    tables = _rope_tables(t)
    for i in range(depth):
        j = i // 2
        if i % 2 == 0:
            x = _pool_layer(x, p["norm_mix"][i], p["pool_w"][j], p["pool_b"][j], p["pool_scale"][j])
            o2d, wo = None, None
        else:
            q, k, vt = _qkv_layer(x, p["norm_mix"][i], p["attn_w_qkv"][j],
                                  p["attn_q_gain"][j], p["attn_k_gain"][j], tables)
            o2d = _attn_layer(q, k, vt, _score_bound(p["attn_q_gain"][j], p["attn_k_gain"][j]))
            o2d = o2d.reshape(bsz * t, d)
            wo = p["attn_w_o"][j]
        x = _ffn_layer(x.reshape(bsz * t, d), o2d, wo, p["norm_ffn"][i],
                       p["ffn_w_gate_up"][i], p["ffn_w_down"][i]).reshape(bsz, t, d)
    return x


def kernel(x_prompt, x_sample, norm_mix, norm_ffn, pool_w, pool_b, pool_scale, attn_w_qkv, attn_q_gain, attn_k_gain, attn_w_o, ffn_w_gate_up, ffn_w_down):
    p = dict(
        norm_mix=norm_mix, norm_ffn=norm_ffn,
        pool_w=pool_w.astype(BF16), pool_b=pool_b, pool_scale=pool_scale,
        attn_w_qkv=attn_w_qkv.astype(BF16),
        attn_q_gain=attn_q_gain * (math.log2(math.e) / math.sqrt(HEAD_DIM)), attn_k_gain=attn_k_gain,
        attn_w_o=attn_w_o.astype(BF16),
        ffn_w_gate_up=ffn_w_gate_up.astype(BF16), ffn_w_down=ffn_w_down.astype(BF16),
    )
    return (_trunk(x_prompt, p), _trunk(x_sample, p))
```

```python
import functools
import math

import jax
import jax.numpy as jnp
from jax import lax
from jax.experimental import pallas as pl
from jax.experimental.pallas import tpu as pltpu

F32 = jnp.float32
BF16 = jnp.bfloat16

POOL_WINDOWS = (2, 4, 8, 16)
HEAD_DIM = 128
N_KV_HEADS = 2
ROPE_AXIS_DIM = HEAD_DIM // 2
ROPE_THETA = 10000.0
GRID_W = 64
EPS = 1e-6

V7X_LANES = 128
V7X_SUBLANES = 8
V7X_VMEM_BYTES = 64 * 1024 * 1024

FFN_ROWS = 512
FFN_COL_CHUNK = 256
QKV_ROWS = 512
QKV_COL_GROUP = 512
ATTN_Q_ROWS = 512
ATTN_KEY_CHUNK = 512
ATTN_EXP2_HEADROOM = 60.0
ATTN_FIXED_OFFSET_MAX_BOUND = 75.0
BF16_ROUNDING_MARGIN = 1.02
ATTN_STAGES = 3
VT_ROWS = HEAD_DIM + 16
POOL_ROWS = 512
POOL_HALO = V7X_SUBLANES
VMEM_LIMIT = V7X_VMEM_BYTES - 8 * 1024 * 1024


def _rms(x, g):
    ms = jnp.mean(x * x, axis=-1, keepdims=True)
    return x * lax.rsqrt(ms + EPS) * g


def _const_spec(shape):
    return pl.BlockSpec(shape, lambda *_: (0,) * len(shape), pipeline_mode=pl.Buffered(1))


def _ffn_kernel(*refs, has_attn, d_ff):
    if has_attn:
        x_ref, o_ref, wo_ref, g_ref, wgu_ref, wd_ref, out_ref, h_scr, a_scr = refs
        out_ref[...] = x_ref[...] + jnp.dot(o_ref[...], wo_ref[...], preferred_element_type=F32)
        res_ref = out_ref
    else:
        x_ref, g_ref, wgu_ref, wd_ref, out_ref, h_scr, a_scr = refs
        res_ref = x_ref
    h_scr[...] = _rms(res_ref[...], g_ref[...]).astype(BF16)
    for c in range(d_ff // FFN_COL_CHUNK):
        lo = c * FFN_COL_CHUNK
        gate = jnp.dot(h_scr[...], wgu_ref[:, lo:lo + FFN_COL_CHUNK], preferred_element_type=F32)
        up = jnp.dot(h_scr[...], wgu_ref[:, d_ff + lo:d_ff + lo + FFN_COL_CHUNK], preferred_element_type=F32)
        a_scr[:, lo:lo + FFN_COL_CHUNK] = (jax.nn.silu(gate) * up).astype(BF16)
    out_ref[...] = res_ref[...] + jnp.dot(a_scr[...], wd_ref[...], preferred_element_type=F32)


def _ffn_layer(x2d, o2d, wo, g, wgu, wd):
    n, d = x2d.shape
    d_ff = wd.shape[0]
    has_attn = o2d is not None
    row_spec = pl.BlockSpec((FFN_ROWS, d), lambda i: (i, 0))
    in_specs = [row_spec]
    args = [x2d]
    if has_attn:
        in_specs += [row_spec, _const_spec(wo.shape)]
        args += [o2d, wo]
    in_specs += [_const_spec((1, d)), _const_spec(wgu.shape), _const_spec(wd.shape)]
    args += [g.reshape(1, d), wgu, wd]
    return pl.pallas_call(
        functools.partial(_ffn_kernel, has_attn=has_attn, d_ff=d_ff),
        grid=(n // FFN_ROWS,),
        in_specs=in_specs,
        out_specs=row_spec,
        out_shape=jax.ShapeDtypeStruct((n, d), F32),
        scratch_shapes=[pltpu.VMEM((FFN_ROWS, d), BF16), pltpu.VMEM((FFN_ROWS, d_ff), BF16)],
        compiler_params=pltpu.CompilerParams(
            dimension_semantics=("arbitrary",), vmem_limit_bytes=VMEM_LIMIT),
        name="ffn_attn" if has_attn else "ffn",
    )(*args)


def _pool_kernel(x_ref, xp_ref, xn_ref, g_ref, w_ref, b_ref, sc_ref, out_ref, hext, *, seq_len):
    i = pl.program_id(1)
    rows = x_ref.shape[0]
    gd = w_ref.shape[1]
    g = g_ref[...]
    hext[POOL_HALO:POOL_HALO + rows, :] = _rms(x_ref[...], g)
    hext[0:POOL_HALO, :] = jnp.where(i > 0, _rms(xp_ref[...], g), 0.0)
    hext[POOL_HALO + rows:, :] = jnp.where(i < pl.num_programs(1) - 1, _rms(xn_ref[...], g), 0.0)
    t = i * rows + lax.broadcasted_iota(jnp.int32, (rows, 1), 0)
    for gi, w in enumerate(POOL_WINDOWS):
        cs = slice(gi * gd, (gi + 1) * gd)
        half = w // 2
        wsum = hext[POOL_HALO - half:POOL_HALO - half + rows, cs]
        for off in range(-half + 1, half):
            wsum = wsum + hext[POOL_HALO + off:POOL_HALO + off + rows, cs]
        count = (jnp.minimum(t + half, seq_len) - jnp.maximum(t - half, 0)).astype(F32)
        pooled = wsum / count - hext[POOL_HALO:POOL_HALO + rows, cs]
        y = jnp.dot(pooled.astype(BF16), w_ref[gi], preferred_element_type=F32) + b_ref[:, cs]
        out_ref[:, cs] = x_ref[:, cs] + y * sc_ref[:, cs]


def _pool_layer(x, g, w, b, sc):
    bsz, t, d = x.shape
    nblk = POOL_ROWS // POOL_HALO
    last = t // POOL_HALO - 1
    tile = pl.BlockSpec((None, POOL_ROWS, d), lambda bi, i: (bi, i, 0))
    prev = pl.BlockSpec((None, POOL_HALO, d), lambda bi, i: (bi, jnp.maximum(i * nblk - 1, 0), 0))
    nxt = pl.BlockSpec((None, POOL_HALO, d), lambda bi, i: (bi, jnp.minimum((i + 1) * nblk, last), 0))
    return pl.pallas_call(
        functools.partial(_pool_kernel, seq_len=t),
        grid=(bsz, t // POOL_ROWS),
        in_specs=[tile, prev, nxt, _const_spec((1, d)), _const_spec(w.shape),
                  _const_spec((1, d)), _const_spec((1, d))],
        out_specs=tile,
        out_shape=jax.ShapeDtypeStruct(x.shape, F32),
        scratch_shapes=[pltpu.VMEM((POOL_ROWS + 2 * POOL_HALO, d), F32)],
        compiler_params=pltpu.CompilerParams(
            dimension_semantics=("arbitrary", "arbitrary"), vmem_limit_bytes=VMEM_LIMIT),
        name="pool",
    )(x, x, x, g.reshape(1, d), w, b.reshape(1, d), sc.reshape(1, d))


def _qkv_kernel(x_ref, g_ref, wt_ref, aq_ref, bq_ref, ak_ref, bk_ref,
                qt_ref, k_ref, vt_ref, h_scr, *, n_heads):
    rows = x_ref.shape[0]
    h_scr[...] = _rms(x_ref[...], g_ref[...]).astype(BF16)
    n_groups = wt_ref.shape[0] // QKV_COL_GROUP
    heads_per_group = QKV_COL_GROUP // HEAD_DIM
    half = HEAD_DIM // 2

    def project(gi):
        return lax.dot_general(wt_ref[gi * QKV_COL_GROUP:(gi + 1) * QKV_COL_GROUP, :], h_scr[...],
                               (((1,), (1,)), ((), ())), preferred_element_type=F32)

    def norm_rope(xt, a_ref, b_ref):
        inv = lax.rsqrt(jnp.mean(xt * xt, axis=0, keepdims=True) + EPS)
        swapped = jnp.concatenate([xt[half:], xt[:half]], axis=0)
        return (xt * a_ref[...] + swapped * b_ref[...]) * inv

    nxt = project(0)
    for gi in range(n_groups):
        cur = nxt
        if gi + 1 < n_groups:
            nxt = project(gi + 1)
        for j in range(heads_per_group):
            hd = gi * heads_per_group + j
            xt = cur[j * HEAD_DIM:(j + 1) * HEAD_DIM, :]
            if hd < n_heads:
                qt_ref[hd] = norm_rope(xt, aq_ref, bq_ref).astype(BF16)
            elif hd < n_heads + N_KV_HEADS:
                k_ref[hd - n_heads] = norm_rope(xt, ak_ref, bk_ref).T.astype(BF16)
            else:
                kh = hd - n_heads - N_KV_HEADS
                vt_ref[kh, :HEAD_DIM, :] = xt.astype(BF16)
                vt_ref[kh, HEAD_DIM:, :] = jnp.ones((VT_ROWS - HEAD_DIM, rows), BF16)


def _rope_layout():
    quarter = ROPE_AXIS_DIM // 2
    idx = jnp.arange(HEAD_DIM)
    blk = idx // quarter
    return jnp.array([0, 2, 1, 3])[blk] * quarter + idx % quarter


def _rope_tables(t):
    pos = jnp.arange(t, dtype=F32)
    row = jnp.floor(pos / GRID_W)
    col = pos - row * GRID_W
    inv_freq = ROPE_THETA ** (-jnp.arange(0, ROPE_AXIS_DIM, 2, dtype=F32) / ROPE_AXIS_DIM)
    ang_r = inv_freq[:, None] * row[None, :]
    ang_c = inv_freq[:, None] * col[None, :]
    ang = jnp.concatenate([ang_r, ang_c, ang_r, ang_c], axis=0)
    sign = jnp.where(jnp.arange(HEAD_DIM) < HEAD_DIM // 2, -1.0, 1.0)
    return jnp.cos(ang), jnp.sin(ang) * sign[:, None]


def _gain_rope_tables(gain, tables):
    cos, sin = tables
    return gain[:, None] * cos, jnp.roll(gain, HEAD_DIM // 2)[:, None] * sin


def _qkv_weight_t(w, n_heads):
    d, e = w.shape
    n_qk = n_heads + N_KV_HEADS
    wh = w.reshape(d, e // HEAD_DIM, HEAD_DIM)
    qk = jnp.take(wh[:, :n_qk], _rope_layout(), axis=2)
    return jnp.concatenate([qk, wh[:, n_qk:]], axis=1).reshape(d, e).T


def _qkv_layer(x, g, wt, q_tabs, k_tabs):
    bsz, t, d = x.shape
    n_heads = d // HEAD_DIM
    tab_spec = pl.BlockSpec((HEAD_DIM, QKV_ROWS), lambda bi, i: (0, i))
    return pl.pallas_call(
        functools.partial(_qkv_kernel, n_heads=n_heads),
        grid=(bsz, t // QKV_ROWS),
        in_specs=[pl.BlockSpec((None, QKV_ROWS, d), lambda bi, i: (bi, i, 0)),
                  _const_spec((1, d)), _const_spec(wt.shape),
                  tab_spec, tab_spec, tab_spec, tab_spec],
        out_specs=[pl.BlockSpec((None, n_heads, HEAD_DIM, QKV_ROWS), lambda bi, i: (bi, 0, 0, i)),
                   pl.BlockSpec((None, N_KV_HEADS, QKV_ROWS, HEAD_DIM), lambda bi, i: (bi, 0, i, 0)),
                   pl.BlockSpec((None, N_KV_HEADS, VT_ROWS, QKV_ROWS), lambda bi, i: (bi, 0, 0, i))],
        out_shape=[jax.ShapeDtypeStruct((bsz, n_heads, HEAD_DIM, t), BF16),
                   jax.ShapeDtypeStruct((bsz, N_KV_HEADS, t, HEAD_DIM), BF16),
                   jax.ShapeDtypeStruct((bsz, N_KV_HEADS, VT_ROWS, t), BF16)],
        scratch_shapes=[pltpu.VMEM((QKV_ROWS, d), BF16)],
        compiler_params=pltpu.CompilerParams(
            dimension_semantics=("arbitrary", "arbitrary"), vmem_limit_bytes=VMEM_LIMIT),
        name="qkv",
    )(x, g.reshape(1, d), wt, *q_tabs, *k_tabs)


def _attn_units(k_ref, group):
    n_chunks = k_ref.shape[0] // ATTN_KEY_CHUNK
    return [(hd, c) for hd in range(group) for c in range(n_chunks)], n_chunks


def _attn_scores(qt_ref, k_ref, hd, c):
    return jnp.dot(k_ref[c * ATTN_KEY_CHUNK:(c + 1) * ATTN_KEY_CHUNK, :], qt_ref[hd],
                   preferred_element_type=F32)


def _attn_store(o_ref, hd, acc):
    out = acc[:HEAD_DIM] / acc[HEAD_DIM:HEAD_DIM + 1]
    o_ref[:, hd * HEAD_DIM:(hd + 1) * HEAD_DIM] = out.T.astype(BF16)


def _attn_fixed_offset(offset, qt_ref, k_ref, vt_ref, o_ref, group):
    units, n_chunks = _attn_units(k_ref, group)

    def probs(u):
        hd, c = units[u]
        return jnp.exp2(_attn_scores(qt_ref, k_ref, hd, c) - offset).astype(BF16)

    p_next = probs(0)
    for u, (hd, c) in enumerate(units):
        p = p_next
        if u + 1 < len(units):
            p_next = probs(u + 1)
        pv = jnp.dot(vt_ref[:, c * ATTN_KEY_CHUNK:(c + 1) * ATTN_KEY_CHUNK], p,
                     preferred_element_type=F32)
        acc = pv if c == 0 else acc + pv
        if c == n_chunks - 1:
            _attn_store(o_ref, hd, acc)


def _attn_online(qt_ref, k_ref, vt_ref, o_ref, st_scr, group):
    rows = qt_ref.shape[-1]
    units, n_chunks = _attn_units(k_ref, group)
    n_units = len(units)
    cmax = {}

    def scores(u):
        st_scr[u % ATTN_STAGES] = _attn_scores(qt_ref, k_ref, *units[u])

    def chunk_max(u):
        cmax[u] = jnp.max(st_scr[u % ATTN_STAGES], axis=0, keepdims=True)

    scores(0)
    scores(1)
    chunk_max(0)
    for u, (hd, c) in enumerate(units):
        if u + 2 < n_units:
            scores(u + 2)
        if u + 1 < n_units:
            chunk_max(u + 1)
        if c == 0:
            m = jnp.full((1, rows), -jnp.inf, F32)
            acc = jnp.zeros((VT_ROWS, rows), F32)
        m_new = jnp.maximum(m, cmax.pop(u))
        alpha = jnp.exp2(m - m_new)
        e = jnp.exp2(st_scr[u % ATTN_STAGES] - m_new)
        acc = alpha * acc + jnp.dot(vt_ref[:, c * ATTN_KEY_CHUNK:(c + 1) * ATTN_KEY_CHUNK],
                                    e.astype(BF16), preferred_element_type=F32)
        m = m_new
        if c == n_chunks - 1:
            _attn_store(o_ref, hd, acc)


def _attn_kernel(bound_ref, qt_ref, k_ref, vt_ref, o_ref, st_scr, *, group):
    bound = bound_ref[0]

    @pl.when(bound <= ATTN_FIXED_OFFSET_MAX_BOUND)
    def _():
        _attn_fixed_offset(bound - ATTN_EXP2_HEADROOM, qt_ref, k_ref, vt_ref, o_ref, group)

    @pl.when(jnp.logical_not(bound <= ATTN_FIXED_OFFSET_MAX_BOUND))
    def _():
        _attn_online(qt_ref, k_ref, vt_ref, o_ref, st_scr, group)


def _score_bound(gq, gk):
    return (HEAD_DIM * BF16_ROUNDING_MARGIN * jnp.max(jnp.abs(gq)) * jnp.max(jnp.abs(gk))).reshape(1)


def _attn_layer(qt, k, vt, bound):
    bsz, n_heads, _, t = qt.shape
    d = n_heads * HEAD_DIM
    group = n_heads // N_KV_HEADS
    gw = group * HEAD_DIM
    o_spec = pl.BlockSpec((None, ATTN_Q_ROWS, gw), lambda bi, kh, i: (bi, i, kh))
    return pl.pallas_call(
        functools.partial(_attn_kernel, group=group),
        grid=(bsz, N_KV_HEADS, t // ATTN_Q_ROWS),
        in_specs=[pl.BlockSpec(memory_space=pltpu.SMEM),
                  pl.BlockSpec((None, group, HEAD_DIM, ATTN_Q_ROWS), lambda bi, kh, i: (bi, kh, 0, i)),
                  pl.BlockSpec((None, None, t, HEAD_DIM), lambda bi, kh, i: (bi, kh, 0, 0)),
                  pl.BlockSpec((None, None, VT_ROWS, t), lambda bi, kh, i: (bi, kh, 0, 0))],
        out_specs=o_spec,
        out_shape=jax.ShapeDtypeStruct((bsz, t, d), BF16),
        scratch_shapes=[pltpu.VMEM((ATTN_STAGES, ATTN_KEY_CHUNK, ATTN_Q_ROWS), F32)],
        compiler_params=pltpu.CompilerParams(
            dimension_semantics=("arbitrary", "arbitrary", "arbitrary"), vmem_limit_bytes=VMEM_LIMIT),
        name="attn",
    )(bound, qt, k, vt)


def _trunk(x, p):
    bsz, t, d = x.shape
    depth = p["norm_mix"].shape[0]
    for i in range(depth):
        j = i // 2
        if i % 2 == 0:
            x = _pool_layer(x, p["norm_mix"][i], p["pool_w"][j], p["pool_b"][j], p["pool_scale"][j])
            o2d, wo = None, None
        else:
            tables = _rope_tables(t)
            qt, k, vt = _qkv_layer(x, p["norm_mix"][i], p["attn_w_qkv_t"][j],
                                   _gain_rope_tables(p["attn_q_gain"][j], tables),
                                   _gain_rope_tables(p["attn_k_gain"][j], tables))
            o2d = _attn_layer(qt, k, vt, _score_bound(p["attn_q_gain"][j], p["attn_k_gain"][j]))
            o2d = o2d.reshape(bsz * t, d)
            wo = p["attn_w_o"][j]
        x = _ffn_layer(x.reshape(bsz * t, d), o2d, wo, p["norm_ffn"][i],
                       p["ffn_w_gate_up"][i], p["ffn_w_down"][i]).reshape(bsz, t, d)
    return x


def kernel(x_prompt, x_sample, norm_mix, norm_ffn, pool_w, pool_b, pool_scale, attn_w_qkv, attn_q_gain, attn_k_gain, attn_w_o, ffn_w_gate_up, ffn_w_down):
    n_heads = x_prompt.shape[-1] // HEAD_DIM
    lanes = _rope_layout()
    p = dict(
        norm_mix=norm_mix, norm_ffn=norm_ffn,
        pool_w=pool_w.astype(BF16), pool_b=pool_b, pool_scale=pool_scale,
        attn_w_qkv_t=jax.vmap(lambda w: _qkv_weight_t(w, n_heads))(attn_w_qkv).astype(BF16),
        attn_q_gain=attn_q_gain[:, lanes] * (math.log2(math.e) / math.sqrt(HEAD_DIM)),
        attn_k_gain=attn_k_gain[:, lanes],
        attn_w_o=attn_w_o.astype(BF16),
        ffn_w_gate_up=ffn_w_gate_up.astype(BF16), ffn_w_down=ffn_w_down.astype(BF16),
    )
    return (_trunk(x_prompt, p), _trunk(x_sample, p))
```

```python
import functools
import math

import jax
import jax.numpy as jnp
from jax import lax
from jax.experimental import pallas as pl
from jax.experimental.pallas import tpu as pltpu

F32 = jnp.float32
BF16 = jnp.bfloat16

POOL_WINDOWS = (2, 4, 8, 16)
HEAD_DIM = 128
N_KV_HEADS = 2
ROPE_AXIS_DIM = HEAD_DIM // 2
ROPE_THETA = 10000.0
GRID_W = 64
EPS = 1e-6

V7X_SUBLANES = 8
V7X_VMEM_BYTES = 64 * 1024 * 1024

FFN_ROWS = 512
FFN_COL_CHUNK = 256
QKV_ROWS = 512
QKV_COL_GROUP = 512
ATTN_Q_ROWS = 512
ATTN_KEY_CHUNK = 512
ATTN_EXP2_HEADROOM = 60.0
ATTN_FIXED_OFFSET_MAX_BOUND = 75.0
BF16_ROUNDING_MARGIN = 1.02
ATTN_STAGES = 3
POOL_ROWS = 512
POOL_HALO = V7X_SUBLANES
VMEM_LIMIT = V7X_VMEM_BYTES - 8 * 1024 * 1024


def _rms(x, g):
    ms = jnp.mean(x * x, axis=-1, keepdims=True)
    return x * lax.rsqrt(ms + EPS) * g


def _layer_spec(stacked, layer):
    tail = stacked.shape[1:]
    return pl.BlockSpec((None,) + tail, lambda *_: (layer,) + (0,) * len(tail),
                        pipeline_mode=pl.Buffered(1))


def _ffn_kernel(*refs, has_attn, d_ff):
    if has_attn:
        x_ref, o_ref, wo_ref, g_ref, wgu_ref, wd_ref, out_ref, h_scr, a_scr = refs
        out_ref[...] = x_ref[...] + jnp.dot(o_ref[...], wo_ref[...], preferred_element_type=F32)
        res_ref = out_ref
    else:
        x_ref, g_ref, wgu_ref, wd_ref, out_ref, h_scr, a_scr = refs
        res_ref = x_ref
    h_scr[...] = _rms(res_ref[...], g_ref[...]).astype(BF16)
    for c in range(d_ff // FFN_COL_CHUNK):
        lo = c * FFN_COL_CHUNK
        gate = jnp.dot(h_scr[...], wgu_ref[:, lo:lo + FFN_COL_CHUNK], preferred_element_type=F32)
        up = jnp.dot(h_scr[...], wgu_ref[:, d_ff + lo:d_ff + lo + FFN_COL_CHUNK], preferred_element_type=F32)
        a_scr[:, lo:lo + FFN_COL_CHUNK] = (jax.nn.silu(gate) * up).astype(BF16)
    out_ref[...] = res_ref[...] + jnp.dot(a_scr[...], wd_ref[...], preferred_element_type=F32)


def _ffn_layer(x2d, o2d, wo, wo_layer, g, wgu, wd, layer):
    n, d = x2d.shape
    d_ff = wd.shape[1]
    has_attn = o2d is not None
    row_spec = pl.BlockSpec((FFN_ROWS, d), lambda i: (i, 0))
    in_specs = [row_spec]
    args = [x2d]
    if has_attn:
        in_specs += [row_spec, _layer_spec(wo, wo_layer)]
        args += [o2d, wo]
    in_specs += [_layer_spec(g, layer), _layer_spec(wgu, layer), _layer_spec(wd, layer)]
    args += [g, wgu, wd]
    return pl.pallas_call(
        functools.partial(_ffn_kernel, has_attn=has_attn, d_ff=d_ff),
        grid=(n // FFN_ROWS,),
        in_specs=in_specs,
        out_specs=row_spec,
        out_shape=jax.ShapeDtypeStruct((n, d), F32),
        scratch_shapes=[pltpu.VMEM((FFN_ROWS, d), BF16), pltpu.VMEM((FFN_ROWS, d_ff), BF16)],
        compiler_params=pltpu.CompilerParams(
            dimension_semantics=("arbitrary",), vmem_limit_bytes=VMEM_LIMIT),
        name="ffn_attn" if has_attn else "ffn",
    )(*args)


def _pool_kernel(x_ref, xp_ref, xn_ref, g_ref, w_ref, b_ref, sc_ref, out_ref, hext, *, seq_len):
    i = pl.program_id(1)
    rows = x_ref.shape[0]
    gd = w_ref.shape[1]
    g = g_ref[...]
    hext[POOL_HALO:POOL_HALO + rows, :] = _rms(x_ref[...], g)
    hext[0:POOL_HALO, :] = jnp.where(i > 0, _rms(xp_ref[...], g), 0.0)
    hext[POOL_HALO + rows:, :] = jnp.where(i < pl.num_programs(1) - 1, _rms(xn_ref[...], g), 0.0)
    ext_rows = rows + 2 * POOL_HALO
    edge = V7X_SUBLANES
    t_top = i * rows + lax.broadcasted_iota(jnp.int32, (edge, 1), 0)
    t_bot = t_top + (rows - edge)

    def ahead(v, k):
        return pltpu.roll(v, ext_rows - k, 0)

    def count(t, half):
        return (jnp.minimum(t + half, seq_len) - jnp.maximum(t - half, 0)).astype(F32)

    for gi, w in enumerate(POOL_WINDOWS):
        cs = slice(gi * gd, (gi + 1) * gd)
        half = w // 2
        run, width = hext[:, cs], 1
        while width < half:
            run = run + ahead(run, width)
            width *= 2
        start = POOL_HALO - half
        wsum = run[start:start + rows] + run[POOL_HALO:POOL_HALO + rows]
        wmean = jnp.concatenate([wsum[:edge] / count(t_top, half),
                                 wsum[edge:rows - edge] * (1.0 / w),
                                 wsum[rows - edge:] / count(t_bot, half)], axis=0)
        pooled = wmean - hext[POOL_HALO:POOL_HALO + rows, cs]
        y = jnp.dot(pooled.astype(BF16), w_ref[gi], preferred_element_type=F32) + b_ref[:, cs]
        out_ref[:, cs] = x_ref[:, cs] + y * sc_ref[:, cs]


def _pool_layer(x, g, layer, w, b, sc, pool_layer):
    bsz, t, d = x.shape
    nblk = POOL_ROWS // POOL_HALO
    last = t // POOL_HALO - 1
    tile = pl.BlockSpec((None, POOL_ROWS, d), lambda bi, i: (bi, i, 0))
    prev = pl.BlockSpec((None, POOL_HALO, d), lambda bi, i: (bi, jnp.maximum(i * nblk - 1, 0), 0))
    nxt = pl.BlockSpec((None, POOL_HALO, d), lambda bi, i: (bi, jnp.minimum((i + 1) * nblk, last), 0))
    return pl.pallas_call(
        functools.partial(_pool_kernel, seq_len=t),
        grid=(bsz, t // POOL_ROWS),
        in_specs=[tile, prev, nxt, _layer_spec(g, layer), _layer_spec(w, pool_layer),
                  _layer_spec(b, pool_layer), _layer_spec(sc, pool_layer)],
        out_specs=tile,
        out_shape=jax.ShapeDtypeStruct(x.shape, F32),
        scratch_shapes=[pltpu.VMEM((POOL_ROWS + 2 * POOL_HALO, d), F32)],
        compiler_params=pltpu.CompilerParams(
            dimension_semantics=("arbitrary", "arbitrary"), vmem_limit_bytes=VMEM_LIMIT),
        name="pool",
    )(x, x, x, g, w, b, sc)


def _qkv_kernel(x_ref, g_ref, wt_ref, aq_ref, bq_ref, ak_ref, bk_ref,
                qt_ref, k_ref, vt_ref, h_scr, *, n_heads):
    h_scr[...] = _rms(x_ref[...], g_ref[...]).astype(BF16)
    n_groups = wt_ref.shape[0] // QKV_COL_GROUP
    heads_per_group = QKV_COL_GROUP // HEAD_DIM
    half = HEAD_DIM // 2

    def project(gi):
        return lax.dot_general(wt_ref[gi * QKV_COL_GROUP:(gi + 1) * QKV_COL_GROUP, :], h_scr[...],
                               (((1,), (1,)), ((), ())), preferred_element_type=F32)

    def norm_rope(xt, a_ref, b_ref):
        inv = lax.rsqrt(jnp.mean(xt * xt, axis=0, keepdims=True) + EPS)
        swapped = jnp.concatenate([xt[half:], xt[:half]], axis=0)
        return (xt * a_ref[...] + swapped * b_ref[...]) * inv

    nxt = project(0)
    for gi in range(n_groups):
        cur = nxt
        if gi + 1 < n_groups:
            nxt = project(gi + 1)
        for j in range(heads_per_group):
            hd = gi * heads_per_group + j
            xt = cur[j * HEAD_DIM:(j + 1) * HEAD_DIM, :]
            if hd < n_heads:
                qt_ref[hd] = norm_rope(xt, aq_ref, bq_ref).astype(BF16)
            elif hd < n_heads + N_KV_HEADS:
                k_ref[hd - n_heads] = norm_rope(xt, ak_ref, bk_ref).T.astype(BF16)
            else:
                vt_ref[hd - n_heads - N_KV_HEADS] = xt.astype(BF16)


def _rope_layout():
    quarter = ROPE_AXIS_DIM // 2
    idx = jnp.arange(HEAD_DIM)
    blk = idx // quarter
    return jnp.array([0, 2, 1, 3])[blk] * quarter + idx % quarter


def _rope_tables(t):
    pos = jnp.arange(t, dtype=F32)
    row = jnp.floor(pos / GRID_W)
    col = pos - row * GRID_W
    inv_freq = ROPE_THETA ** (-jnp.arange(0, ROPE_AXIS_DIM, 2, dtype=F32) / ROPE_AXIS_DIM)
    ang_r = inv_freq[:, None] * row[None, :]
    ang_c = inv_freq[:, None] * col[None, :]
    ang = jnp.concatenate([ang_r, ang_c, ang_r, ang_c], axis=0)
    sign = jnp.where(jnp.arange(HEAD_DIM) < HEAD_DIM // 2, -1.0, 1.0)
    return jnp.cos(ang), jnp.sin(ang) * sign[:, None]


def _gain_rope_tables(gain, tables):
    cos, sin = tables
    return gain[:, None] * cos, jnp.roll(gain, HEAD_DIM // 2)[:, None] * sin


def _qkv_weight_t(w, n_heads):
    d, e = w.shape
    n_qk = n_heads + N_KV_HEADS
    wh = w.reshape(d, e // HEAD_DIM, HEAD_DIM)
    qk = jnp.take(wh[:, :n_qk], _rope_layout(), axis=2)
    return jnp.concatenate([qk, wh[:, n_qk:]], axis=1).reshape(d, e).T


def _qkv_layer(x, g, layer, wt, attn_layer, q_tabs, k_tabs):
    bsz, t, d = x.shape
    n_heads = d // HEAD_DIM
    tab_spec = pl.BlockSpec((HEAD_DIM, QKV_ROWS), lambda bi, i: (0, i))
    return pl.pallas_call(
        functools.partial(_qkv_kernel, n_heads=n_heads),
        grid=(bsz, t // QKV_ROWS),
        in_specs=[pl.BlockSpec((None, QKV_ROWS, d), lambda bi, i: (bi, i, 0)),
                  _layer_spec(g, layer), _layer_spec(wt, attn_layer),
                  tab_spec, tab_spec, tab_spec, tab_spec],
        out_specs=[pl.BlockSpec((None, n_heads, HEAD_DIM, QKV_ROWS), lambda bi, i: (bi, 0, 0, i)),
                   pl.BlockSpec((None, N_KV_HEADS, QKV_ROWS, HEAD_DIM), lambda bi, i: (bi, 0, i, 0)),
                   pl.BlockSpec((None, N_KV_HEADS, HEAD_DIM, QKV_ROWS), lambda bi, i: (bi, 0, 0, i))],
        out_shape=[jax.ShapeDtypeStruct((bsz, n_heads, HEAD_DIM, t), BF16),
                   jax.ShapeDtypeStruct((bsz, N_KV_HEADS, t, HEAD_DIM), BF16),
                   jax.ShapeDtypeStruct((bsz, N_KV_HEADS, HEAD_DIM, t), BF16)],
        scratch_shapes=[pltpu.VMEM((QKV_ROWS, d), BF16)],
        compiler_params=pltpu.CompilerParams(
            dimension_semantics=("arbitrary", "arbitrary"), vmem_limit_bytes=VMEM_LIMIT),
        name="qkv",
    )(x, g, wt, *q_tabs, *k_tabs)


def _attn_units(k_ref, group):
    n_chunks = k_ref.shape[0] // ATTN_KEY_CHUNK
    return [(hd, c) for hd in range(group) for c in range(n_chunks)], n_chunks


def _attn_scores(qt_ref, k_ref, hd, c):
    return jnp.dot(k_ref[c * ATTN_KEY_CHUNK:(c + 1) * ATTN_KEY_CHUNK, :], qt_ref[hd],
                   preferred_element_type=F32)


def _attn_store(o_ref, hd, acc, denom):
    o_ref[:, hd * HEAD_DIM:(hd + 1) * HEAD_DIM] = (acc / denom).T.astype(BF16)


def _attn_fixed_offset(offset, qt_ref, k_ref, vt_ref, o_ref, group):
    units, n_chunks = _attn_units(k_ref, group)

    def probs(u):
        hd, c = units[u]
        e = jnp.exp2(_attn_scores(qt_ref, k_ref, hd, c) - offset)
        return e.astype(BF16), jnp.sum(e, axis=0, keepdims=True)

    nxt = probs(0)
    for u, (hd, c) in enumerate(units):
        p, esum = nxt
        if u + 1 < len(units):
            nxt = probs(u + 1)
        pv = jnp.dot(vt_ref[:, c * ATTN_KEY_CHUNK:(c + 1) * ATTN_KEY_CHUNK], p,
                     preferred_element_type=F32)
        acc, denom = (pv, esum) if c == 0 else (acc + pv, denom + esum)
        if c == n_chunks - 1:
            _attn_store(o_ref, hd, acc, denom)


def _attn_online(qt_ref, k_ref, vt_ref, o_ref, st_scr, group):
    rows = qt_ref.shape[-1]
    units, n_chunks = _attn_units(k_ref, group)
    n_units = len(units)
    cmax = {}

    def scores(u):
        st_scr[u % ATTN_STAGES] = _attn_scores(qt_ref, k_ref, *units[u])

    def chunk_max(u):
        cmax[u] = jnp.max(st_scr[u % ATTN_STAGES], axis=0, keepdims=True)

    scores(0)
    scores(1)
    chunk_max(0)
    for u, (hd, c) in enumerate(units):
        if u + 2 < n_units:
            scores(u + 2)
        if u + 1 < n_units:
            chunk_max(u + 1)
        if c == 0:
            m = jnp.full((1, rows), -jnp.inf, F32)
            denom = jnp.zeros((1, rows), F32)
            acc = jnp.zeros((HEAD_DIM, rows), F32)
        m_new = jnp.maximum(m, cmax.pop(u))
        alpha = jnp.exp2(m - m_new)
        e = jnp.exp2(st_scr[u % ATTN_STAGES] - m_new)
        denom = alpha * denom + jnp.sum(e, axis=0, keepdims=True)
        acc = alpha * acc + jnp.dot(vt_ref[:, c * ATTN_KEY_CHUNK:(c + 1) * ATTN_KEY_CHUNK],
                                    e.astype(BF16), preferred_element_type=F32)
        m = m_new
        if c == n_chunks - 1:
            _attn_store(o_ref, hd, acc, denom)


def _attn_kernel(bound_ref, qt_ref, k_ref, vt_ref, o_ref, st_scr, *, group):
    bound = bound_ref[0]

    @pl.when(bound <= ATTN_FIXED_OFFSET_MAX_BOUND)
    def _():
        _attn_fixed_offset(bound - ATTN_EXP2_HEADROOM, qt_ref, k_ref, vt_ref, o_ref, group)

    @pl.when(jnp.logical_not(bound <= ATTN_FIXED_OFFSET_MAX_BOUND))
    def _():
        _attn_online(qt_ref, k_ref, vt_ref, o_ref, st_scr, group)


def _score_bound(gq, gk):
    return (HEAD_DIM * BF16_ROUNDING_MARGIN * jnp.max(jnp.abs(gq)) * jnp.max(jnp.abs(gk))).reshape(1)


def _attn_layer(qt, k, vt, bound):
    bsz, n_heads, _, t = qt.shape
    d = n_heads * HEAD_DIM
    group = n_heads // N_KV_HEADS
    gw = group * HEAD_DIM
    o_spec = pl.BlockSpec((None, ATTN_Q_ROWS, gw), lambda bi, kh, i: (bi, i, kh))
    return pl.pallas_call(
        functools.partial(_attn_kernel, group=group),
        grid=(bsz, N_KV_HEADS, t // ATTN_Q_ROWS),
        in_specs=[pl.BlockSpec(memory_space=pltpu.SMEM),
                  pl.BlockSpec((None, group, HEAD_DIM, ATTN_Q_ROWS), lambda bi, kh, i: (bi, kh, 0, i)),
                  pl.BlockSpec((None, None, t, HEAD_DIM), lambda bi, kh, i: (bi, kh, 0, 0)),
                  pl.BlockSpec((None, None, HEAD_DIM, t), lambda bi, kh, i: (bi, kh, 0, 0))],
        out_specs=o_spec,
        out_shape=jax.ShapeDtypeStruct((bsz, t, d), BF16),
        scratch_shapes=[pltpu.VMEM((ATTN_STAGES, ATTN_KEY_CHUNK, ATTN_Q_ROWS), F32)],
        compiler_params=pltpu.CompilerParams(
            dimension_semantics=("arbitrary", "arbitrary", "arbitrary"), vmem_limit_bytes=VMEM_LIMIT),
        name="attn",
    )(bound, qt, k, vt)


def _trunk(x, p):
    bsz, t, d = x.shape
    depth = p["norm_mix"].shape[0]
    for i in range(depth):
        j = i // 2
        if i % 2 == 0:
            x = _pool_layer(x, p["norm_mix"], i, p["pool_w"], p["pool_b"], p["pool_scale"], j)
            o2d = None
        else:
            qt, k, vt = _qkv_layer(x, p["norm_mix"], i, p["attn_w_qkv_t"], j,
                                   p["q_tables"][j], p["k_tables"][j])
            o2d = _attn_layer(qt, k, vt, p["score_bound"][j]).reshape(bsz * t, d)
        x = _ffn_layer(x.reshape(bsz * t, d), o2d, p["attn_w_o"], j, p["norm_ffn"],
                       p["ffn_w_gate_up"], p["ffn_w_down"], i).reshape(bsz, t, d)
    return x


def kernel(x_prompt, x_sample, norm_mix, norm_ffn, pool_w, pool_b, pool_scale, attn_w_qkv, attn_q_gain, attn_k_gain, attn_w_o, ffn_w_gate_up, ffn_w_down):
    assert x_prompt.shape[1:] == x_sample.shape[1:]
    _, t, d = x_prompt.shape
    n_heads = d // HEAD_DIM
    n_attn = attn_w_qkv.shape[0]
    lanes = _rope_layout()
    gq = attn_q_gain[:, lanes] * (math.log2(math.e) / math.sqrt(HEAD_DIM))
    gk = attn_k_gain[:, lanes]
    tables = _rope_tables(t)
    p = dict(
        norm_mix=norm_mix[:, None, :], norm_ffn=norm_ffn[:, None, :],
        pool_w=pool_w.astype(BF16), pool_b=pool_b.reshape(pool_b.shape[0], 1, d),
        pool_scale=pool_scale[:, None, :],
        attn_w_qkv_t=jax.vmap(lambda w: _qkv_weight_t(w, n_heads))(attn_w_qkv).astype(BF16),
        q_tables=[_gain_rope_tables(gq[j], tables) for j in range(n_attn)],
        k_tables=[_gain_rope_tables(gk[j], tables) for j in range(n_attn)],
        score_bound=[_score_bound(gq[j], gk[j]) for j in range(n_attn)],
        attn_w_o=attn_w_o.astype(BF16),
        ffn_w_gate_up=ffn_w_gate_up.astype(BF16), ffn_w_down=ffn_w_down.astype(BF16),
    )
    return (_trunk(x_prompt, p), _trunk(x_sample, p))
```

```python
import functools
import math

import jax
import jax.numpy as jnp
from jax import lax
from jax.experimental import pallas as pl
from jax.experimental.pallas import tpu as pltpu

F32 = jnp.float32
BF16 = jnp.bfloat16

POOL_WINDOWS = (2, 4, 8, 16)
HEAD_DIM = 128
N_KV_HEADS = 2
ROPE_AXIS_DIM = HEAD_DIM // 2
ROPE_THETA = 10000.0
GRID_W = 64
EPS = 1e-6

V7X_SUBLANES = 8
V7X_VMEM_BYTES = 64 * 1024 * 1024

FFN_ROWS = 512
FFN_COL_CHUNK = 256
QKV_ROWS = 512
QKV_COL_GROUP = 512
ATTN_Q_ROWS = 512
ATTN_KEY_CHUNK = 512
ATTN_EXP2_HEADROOM = 60.0
ATTN_FIXED_OFFSET_MAX_BOUND = 75.0
BF16_ROUNDING_MARGIN = 1.02
ATTN_STAGES = 3
POOL_HALO = V7X_SUBLANES
VMEM_LIMIT = V7X_VMEM_BYTES - 8 * 1024 * 1024


def _rms(x, g):
    ms = jnp.mean(x * x, axis=-1, keepdims=True)
    return x * lax.rsqrt(ms + EPS) * g


def _layer_spec(stacked, layer):
    tail = stacked.shape[1:]
    return pl.BlockSpec((None,) + tail, lambda *_: (layer,) + (0,) * len(tail),
                        pipeline_mode=pl.Buffered(1))


def _pool_norm(x_ref, before_ref, after_ref, tile, g_mix_ref, hext, *, seq_len):
    rows = x_ref.shape[0]
    tiles_per_seq = seq_len // rows
    kk = tile % tiles_per_seq
    g = g_mix_ref[...]
    hext[POOL_HALO:POOL_HALO + rows, :] = _rms(x_ref[...], g)
    hext[0:POOL_HALO, :] = jnp.where(kk > 0, _rms(before_ref[...], g), 0.0)
    hext[POOL_HALO + rows:, :] = jnp.where(kk < tiles_per_seq - 1, _rms(after_ref[...], g), 0.0)


def _pool_group(gi, x_ref, tile, prm, x1_dst, hext, *, seq_len):
    _, w_ref, b_ref, sc_ref, _ = prm
    rows = x_ref.shape[0]
    gd = w_ref.shape[1]
    kk = tile % (seq_len // rows)
    ext_rows = rows + 2 * POOL_HALO
    edge = V7X_SUBLANES
    t_top = kk * rows + lax.broadcasted_iota(jnp.int32, (edge, 1), 0)
    t_bot = t_top + (rows - edge)
    w = POOL_WINDOWS[gi]
    half = w // 2
    cs = slice(gi * gd, (gi + 1) * gd)

    def count(t):
        return (jnp.minimum(t + half, seq_len) - jnp.maximum(t - half, 0)).astype(F32)

    run, width = hext[:, cs], 1
    while width < half:
        run = run + pltpu.roll(run, ext_rows - width, 0)
        width *= 2
    start = POOL_HALO - half
    wsum = run[start:start + rows] + run[POOL_HALO:POOL_HALO + rows]
    wmean = jnp.concatenate([wsum[:edge] / count(t_top),
                             wsum[edge:rows - edge] * (1.0 / w),
                             wsum[rows - edge:] / count(t_bot)], axis=0)
    pooled = wmean - hext[POOL_HALO:POOL_HALO + rows, cs]
    y = jnp.dot(pooled.astype(BF16), w_ref[gi], preferred_element_type=F32) + b_ref[:, cs]
    x1_dst[:, cs] = x_ref[:, cs] + y * sc_ref[:, cs]


def _attn_prologue(x_ref, o_ref, prm, x1_dst, h_dst):
    wo_ref, g_ffn_ref = prm
    x1_dst[...] = x_ref[...] + jnp.dot(o_ref[...], wo_ref[...], preferred_element_type=F32)
    h_dst[...] = _rms(x1_dst[...], g_ffn_ref[...]).astype(BF16)


def _mix_ffn_kernel(*refs, mode, d_ff, seq_len):
    n_in = {"pool": 5, "attn": 4}[mode]
    n_prm = {"pool": 5, "attn": 2}[mode]
    tile_refs, prm = refs[:n_in], refs[n_in:n_in + n_prm]
    wgu_ref, wd_ref, out_ref = refs[n_in + n_prm:n_in + n_prm + 3]
    h_cur, h_next, x1_cur, x1_next, a_scr = refs[n_in + n_prm + 3:n_in + n_prm + 8]
    i = pl.program_id(0)

    n_chunks = d_ff // FFN_COL_CHUNK
    if mode == "pool":
        x_first, after_first, x_la, before_la, after_la = tile_refs
        hext = refs[-1]
        g_ffn_ref = prm[-1]
        n_groups = len(POOL_WINDOWS)

        def first():
            _pool_norm(x_first, after_first, after_first, 0, prm[0], hext, seq_len=seq_len)
            for gi in range(n_groups):
                _pool_group(gi, x_first, 0, prm, x1_cur, hext, seq_len=seq_len)
            h_cur[...] = _rms(x1_cur[...], g_ffn_ref[...]).astype(BF16)

        ahead = {0: [functools.partial(_pool_norm, x_la, before_la, after_la, i + 1, prm[0], hext,
                                       seq_len=seq_len)]}
        for gi in range(n_groups):
            ahead.setdefault(2 * gi + 1, []).append(
                functools.partial(_pool_group, gi, x_la, i + 1, prm, x1_next, hext, seq_len=seq_len))

        def finish():
            h_next[...] = _rms(x1_next[...], g_ffn_ref[...]).astype(BF16)
        ahead.setdefault(2 * n_groups, []).append(finish)
        assert max(ahead) < n_chunks
    else:
        x_first, o_first, x_la, o_la = tile_refs

        def first():
            _attn_prologue(x_first, o_first, prm, x1_cur, h_cur)

        ahead = {0: [functools.partial(_attn_prologue, x_la, o_la, prm, x1_next, h_next)]}

    @pl.when(i == 0)
    def _():
        first()

    @pl.when(i > 0)
    def _():
        h_cur[...] = h_next[...]
        x1_cur[...] = x1_next[...]

    for c in range(n_chunks):
        for step in ahead.get(c, ()):
            step()
        lo = c * FFN_COL_CHUNK
        gate = jnp.dot(h_cur[...], wgu_ref[:, lo:lo + FFN_COL_CHUNK], preferred_element_type=F32)
        up = jnp.dot(h_cur[...], wgu_ref[:, d_ff + lo:d_ff + lo + FFN_COL_CHUNK], preferred_element_type=F32)
        a_scr[:, lo:lo + FFN_COL_CHUNK] = (jax.nn.silu(gate) * up).astype(BF16)
    out_ref[...] = x1_cur[...] + jnp.dot(a_scr[...], wd_ref[...], preferred_element_type=F32)


def _mix_ffn_layer(mode, x2d, seq_len, mix_args, mix_specs, g_ffn, wgu, wd, layer, o2d=None):
    n, d = x2d.shape
    d_ff = wd.shape[1]
    n_tiles = n // FFN_ROWS
    blk = FFN_ROWS // POOL_HALO
    const = dict(pipeline_mode=pl.Buffered(1))
    first = pl.BlockSpec((FFN_ROWS, d), lambda i: (0, 0), **const)
    look = pl.BlockSpec((FFN_ROWS, d), lambda i: (jnp.minimum(i + 1, n_tiles - 1), 0))
    if mode == "pool":
        tile_specs = [first,
                      pl.BlockSpec((POOL_HALO, d), lambda i: (blk, 0), **const),
                      look,
                      pl.BlockSpec((POOL_HALO, d), lambda i: (jnp.minimum(i + 1, n_tiles - 1) * blk - 1, 0)),
                      pl.BlockSpec((POOL_HALO, d), lambda i: (jnp.minimum((i + 2) * blk, n_tiles * blk - 1), 0))]
        tile_args = [x2d] * 5
        extra_scratch = [pltpu.VMEM((FFN_ROWS + 2 * POOL_HALO, d), F32)]
    else:
        tile_specs = [first, first, look, look]
        tile_args = [x2d, o2d, x2d, o2d]
        extra_scratch = []
    return pl.pallas_call(
        functools.partial(_mix_ffn_kernel, mode=mode, d_ff=d_ff, seq_len=seq_len),
        grid=(n_tiles,),
        in_specs=tile_specs + mix_specs + [_layer_spec(g_ffn, layer), _layer_spec(wgu, layer),
                                           _layer_spec(wd, layer)],
        out_specs=pl.BlockSpec((FFN_ROWS, d), lambda i: (i, 0)),
        out_shape=jax.ShapeDtypeStruct((n, d), F32),
        scratch_shapes=[pltpu.VMEM((FFN_ROWS, d), BF16), pltpu.VMEM((FFN_ROWS, d), BF16),
                        pltpu.VMEM((FFN_ROWS, d), F32), pltpu.VMEM((FFN_ROWS, d), F32),
                        pltpu.VMEM((FFN_ROWS, d_ff), BF16)] + extra_scratch,
        compiler_params=pltpu.CompilerParams(
            dimension_semantics=("arbitrary",), vmem_limit_bytes=VMEM_LIMIT),
        name=mode + "_ffn",
    )(*tile_args, *mix_args, g_ffn, wgu, wd)


def _qkv_kernel(x_ref, g_ref, wt_ref, aq_ref, bq_ref, ak_ref, bk_ref,
                qt_ref, k_ref, vt_ref, h_scr, *, n_heads):
    h_scr[...] = _rms(x_ref[...], g_ref[...]).astype(BF16)
    n_groups = wt_ref.shape[0] // QKV_COL_GROUP
    heads_per_group = QKV_COL_GROUP // HEAD_DIM
    half = HEAD_DIM // 2

    def project(gi):
        return lax.dot_general(wt_ref[gi * QKV_COL_GROUP:(gi + 1) * QKV_COL_GROUP, :], h_scr[...],
                               (((1,), (1,)), ((), ())), preferred_element_type=F32)

    def norm_rope(xt, a_ref, b_ref):
        inv = lax.rsqrt(jnp.mean(xt * xt, axis=0, keepdims=True) + EPS)
        swapped = jnp.concatenate([xt[half:], xt[:half]], axis=0)
        return (xt * a_ref[...] + swapped * b_ref[...]) * inv

    nxt = project(0)
    for gi in range(n_groups):
        cur = nxt
        if gi + 1 < n_groups:
            nxt = project(gi + 1)
        for j in range(heads_per_group):
            hd = gi * heads_per_group + j
            xt = cur[j * HEAD_DIM:(j + 1) * HEAD_DIM, :]
            if hd < n_heads:
                qt_ref[hd] = norm_rope(xt, aq_ref, bq_ref).astype(BF16)
            elif hd < n_heads + N_KV_HEADS:
                k_ref[hd - n_heads] = norm_rope(xt, ak_ref, bk_ref).T.astype(BF16)
            else:
                vt_ref[hd - n_heads - N_KV_HEADS] = xt.astype(BF16)


def _rope_layout():
    quarter = ROPE_AXIS_DIM // 2
    idx = jnp.arange(HEAD_DIM)
    blk = idx // quarter
    return jnp.array([0, 2, 1, 3])[blk] * quarter + idx % quarter


def _rope_tables(t):
    pos = jnp.arange(t, dtype=F32)
    row = jnp.floor(pos / GRID_W)
    col = pos - row * GRID_W
    inv_freq = ROPE_THETA ** (-jnp.arange(0, ROPE_AXIS_DIM, 2, dtype=F32) / ROPE_AXIS_DIM)
    ang_r = inv_freq[:, None] * row[None, :]
    ang_c = inv_freq[:, None] * col[None, :]
    ang = jnp.concatenate([ang_r, ang_c, ang_r, ang_c], axis=0)
    sign = jnp.where(jnp.arange(HEAD_DIM) < HEAD_DIM // 2, -1.0, 1.0)
    return jnp.cos(ang), jnp.sin(ang) * sign[:, None]


def _gain_rope_tables(gain, tables):
    cos, sin = tables
    return gain[:, None] * cos, jnp.roll(gain, HEAD_DIM // 2)[:, None] * sin


def _qkv_weight_t(w, n_heads):
    d, e = w.shape
    n_qk = n_heads + N_KV_HEADS
    wh = w.reshape(d, e // HEAD_DIM, HEAD_DIM)
    qk = jnp.take(wh[:, :n_qk], _rope_layout(), axis=2)
    return jnp.concatenate([qk, wh[:, n_qk:]], axis=1).reshape(d, e).T


def _qkv_layer(x, g, layer, wt, attn_layer, q_tabs, k_tabs):
    bsz, t, d = x.shape
    n_heads = d // HEAD_DIM
    tab_spec = pl.BlockSpec((HEAD_DIM, QKV_ROWS), lambda bi, i: (0, i))
    return pl.pallas_call(
        functools.partial(_qkv_kernel, n_heads=n_heads),
        grid=(bsz, t // QKV_ROWS),
        in_specs=[pl.BlockSpec((None, QKV_ROWS, d), lambda bi, i: (bi, i, 0)),
                  _layer_spec(g, layer), _layer_spec(wt, attn_layer),
                  tab_spec, tab_spec, tab_spec, tab_spec],
        out_specs=[pl.BlockSpec((None, n_heads, HEAD_DIM, QKV_ROWS), lambda bi, i: (bi, 0, 0, i)),
                   pl.BlockSpec((None, N_KV_HEADS, QKV_ROWS, HEAD_DIM), lambda bi, i: (bi, 0, i, 0)),
                   pl.BlockSpec((None, N_KV_HEADS, HEAD_DIM, QKV_ROWS), lambda bi, i: (bi, 0, 0, i))],
        out_shape=[jax.ShapeDtypeStruct((bsz, n_heads, HEAD_DIM, t), BF16),
                   jax.ShapeDtypeStruct((bsz, N_KV_HEADS, t, HEAD_DIM), BF16),
                   jax.ShapeDtypeStruct((bsz, N_KV_HEADS, HEAD_DIM, t), BF16)],
        scratch_shapes=[pltpu.VMEM((QKV_ROWS, d), BF16)],
        compiler_params=pltpu.CompilerParams(
            dimension_semantics=("arbitrary", "arbitrary"), vmem_limit_bytes=VMEM_LIMIT),
        name="qkv",
    )(x, g, wt, *q_tabs, *k_tabs)


def _attn_units(k_ref, group):
    n_chunks = k_ref.shape[0] // ATTN_KEY_CHUNK
    return [(hd, c) for hd in range(group) for c in range(n_chunks)], n_chunks


def _attn_scores(qt_ref, k_ref, hd, c):
    return jnp.dot(k_ref[c * ATTN_KEY_CHUNK:(c + 1) * ATTN_KEY_CHUNK, :], qt_ref[hd],
                   preferred_element_type=F32)


def _attn_store(o_ref, hd, acc, denom):
    o_ref[:, hd * HEAD_DIM:(hd + 1) * HEAD_DIM] = (acc / denom).T.astype(BF16)


def _attn_fixed_offset(offset, qt_ref, k_ref, vt_ref, o_ref, group):
    units, n_chunks = _attn_units(k_ref, group)

    def probs(u):
        hd, c = units[u]
        e = jnp.exp2(_attn_scores(qt_ref, k_ref, hd, c) - offset)
        return e.astype(BF16), jnp.sum(e, axis=0, keepdims=True)

    nxt = probs(0)
    for u, (hd, c) in enumerate(units):
        p, esum = nxt
        if u + 1 < len(units):
            nxt = probs(u + 1)
        pv = jnp.dot(vt_ref[:, c * ATTN_KEY_CHUNK:(c + 1) * ATTN_KEY_CHUNK], p,
                     preferred_element_type=F32)
        acc, denom = (pv, esum) if c == 0 else (acc + pv, denom + esum)
        if c == n_chunks - 1:
            _attn_store(o_ref, hd, acc, denom)


def _attn_online(qt_ref, k_ref, vt_ref, o_ref, st_scr, group):
    rows = qt_ref.shape[-1]
    units, n_chunks = _attn_units(k_ref, group)
    n_units = len(units)
    cmax = {}

    def scores(u):
        st_scr[u % ATTN_STAGES] = _attn_scores(qt_ref, k_ref, *units[u])

    def chunk_max(u):
        cmax[u] = jnp.max(st_scr[u % ATTN_STAGES], axis=0, keepdims=True)

    scores(0)
    scores(1)
    chunk_max(0)
    for u, (hd, c) in enumerate(units):
        if u + 2 < n_units:
            scores(u + 2)
        if u + 1 < n_units:
            chunk_max(u + 1)
        if c == 0:
            m = jnp.full((1, rows), -jnp.inf, F32)
            denom = jnp.zeros((1, rows), F32)
            acc = jnp.zeros((HEAD_DIM, rows), F32)
        m_new = jnp.maximum(m, cmax.pop(u))
        alpha = jnp.exp2(m - m_new)
        e = jnp.exp2(st_scr[u % ATTN_STAGES] - m_new)
        denom = alpha * denom + jnp.sum(e, axis=0, keepdims=True)
        acc = alpha * acc + jnp.dot(vt_ref[:, c * ATTN_KEY_CHUNK:(c + 1) * ATTN_KEY_CHUNK],
                                    e.astype(BF16), preferred_element_type=F32)
        m = m_new
        if c == n_chunks - 1:
            _attn_store(o_ref, hd, acc, denom)


def _attn_kernel(bound_ref, qt_ref, k_ref, vt_ref, o_ref, st_scr, *, group):
    bound = bound_ref[0]

    @pl.when(bound <= ATTN_FIXED_OFFSET_MAX_BOUND)
    def _():
        _attn_fixed_offset(bound - ATTN_EXP2_HEADROOM, qt_ref, k_ref, vt_ref, o_ref, group)

    @pl.when(jnp.logical_not(bound <= ATTN_FIXED_OFFSET_MAX_BOUND))
    def _():
        _attn_online(qt_ref, k_ref, vt_ref, o_ref, st_scr, group)


def _score_bound(gq, gk):
    return (HEAD_DIM * BF16_ROUNDING_MARGIN * jnp.max(jnp.abs(gq)) * jnp.max(jnp.abs(gk))).reshape(1)


def _attn_layer(qt, k, vt, bound):
    bsz, n_heads, _, t = qt.shape
    d = n_heads * HEAD_DIM
    group = n_heads // N_KV_HEADS
    gw = group * HEAD_DIM
    o_spec = pl.BlockSpec((None, ATTN_Q_ROWS, gw), lambda bi, kh, i: (bi, i, kh))
    return pl.pallas_call(
        functools.partial(_attn_kernel, group=group),
        grid=(bsz, N_KV_HEADS, t // ATTN_Q_ROWS),
        in_specs=[pl.BlockSpec(memory_space=pltpu.SMEM),
                  pl.BlockSpec((None, group, HEAD_DIM, ATTN_Q_ROWS), lambda bi, kh, i: (bi, kh, 0, i)),
                  pl.BlockSpec((None, None, t, HEAD_DIM), lambda bi, kh, i: (bi, kh, 0, 0)),
                  pl.BlockSpec((None, None, HEAD_DIM, t), lambda bi, kh, i: (bi, kh, 0, 0))],
        out_specs=o_spec,
        out_shape=jax.ShapeDtypeStruct((bsz, t, d), BF16),
        scratch_shapes=[pltpu.VMEM((ATTN_STAGES, ATTN_KEY_CHUNK, ATTN_Q_ROWS), F32)],
        compiler_params=pltpu.CompilerParams(
            dimension_semantics=("arbitrary", "arbitrary", "arbitrary"), vmem_limit_bytes=VMEM_LIMIT),
        name="attn",
    )(bound, qt, k, vt)


def _trunk(x, p):
    bsz, t, d = x.shape
    depth = p["norm_mix"].shape[0]
    x2d = x.reshape(bsz * t, d)
    for i in range(depth):
        j = i // 2
        ffn = (p["norm_ffn"], p["ffn_w_gate_up"], p["ffn_w_down"], i)
        if i % 2 == 0:
            mix = [p["norm_mix"], p["pool_w"], p["pool_b"], p["pool_scale"]]
            specs = [_layer_spec(mix[0], i)] + [_layer_spec(m, j) for m in mix[1:]]
            x2d = _mix_ffn_layer("pool", x2d, t, mix, specs, *ffn)
        else:
            qt, k, vt = _qkv_layer(x2d.reshape(bsz, t, d), p["norm_mix"], i, p["attn_w_qkv_t"], j,
                                   p["q_tables"][j], p["k_tables"][j])
            o2d = _attn_layer(qt, k, vt, p["score_bound"][j]).reshape(bsz * t, d)
            x2d = _mix_ffn_layer("attn", x2d, t, [p["attn_w_o"]], [_layer_spec(p["attn_w_o"], j)],
                                 *ffn, o2d=o2d)
    return x2d.reshape(bsz, t, d)


def kernel(x_prompt, x_sample, norm_mix, norm_ffn, pool_w, pool_b, pool_scale, attn_w_qkv, attn_q_gain, attn_k_gain, attn_w_o, ffn_w_gate_up, ffn_w_down):
    assert x_prompt.shape[1:] == x_sample.shape[1:]
    _, t, d = x_prompt.shape
    n_heads = d // HEAD_DIM
    n_attn = attn_w_qkv.shape[0]
    lanes = _rope_layout()
    gq = attn_q_gain[:, lanes] * (math.log2(math.e) / math.sqrt(HEAD_DIM))
    gk = attn_k_gain[:, lanes]
    tables = _rope_tables(t)
    p = dict(
        norm_mix=norm_mix[:, None, :], norm_ffn=norm_ffn[:, None, :],
        pool_w=pool_w.astype(BF16), pool_b=pool_b.reshape(pool_b.shape[0], 1, d),
        pool_scale=pool_scale[:, None, :],
        attn_w_qkv_t=jax.vmap(lambda w: _qkv_weight_t(w, n_heads))(attn_w_qkv).astype(BF16),
        q_tables=[_gain_rope_tables(gq[j], tables) for j in range(n_attn)],
        k_tables=[_gain_rope_tables(gk[j], tables) for j in range(n_attn)],
        score_bound=[_score_bound(gq[j], gk[j]) for j in range(n_attn)],
        attn_w_o=attn_w_o.astype(BF16),
        ffn_w_gate_up=ffn_w_gate_up.astype(BF16), ffn_w_down=ffn_w_down.astype(BF16),
    )
    return (_trunk(x_prompt, p), _trunk(x_sample, p))
```

```python
import functools
import math

import jax
import jax.numpy as jnp
from jax import lax
from jax.experimental import pallas as pl
from jax.experimental.pallas import tpu as pltpu

F32 = jnp.float32
BF16 = jnp.bfloat16

POOL_WINDOWS = (2, 4, 8, 16)
HEAD_DIM = 128
N_KV_HEADS = 2
ROPE_AXIS_DIM = HEAD_DIM // 2
ROPE_THETA = 10000.0
GRID_W = 64
EPS = 1e-6

V7X_SUBLANES = 8
V7X_VMEM_BYTES = 64 * 1024 * 1024

FFN_ROWS = 512
FFN_COL_CHUNK = 256
QKV_ROWS = 512
QKV_COL_GROUP = 512
ATTN_Q_ROWS = 512
ATTN_KEY_CHUNK = 512
ATTN_EXP2_HEADROOM = 60.0
ATTN_FIXED_OFFSET_MAX_BOUND = 75.0
BF16_ROUNDING_MARGIN = 1.02
ATTN_STAGES = 3
POOL_HALO = V7X_SUBLANES
VMEM_LIMIT = V7X_VMEM_BYTES - 8 * 1024 * 1024


def _rms(x, g):
    ms = jnp.mean(x * x, axis=-1, keepdims=True)
    return x * lax.rsqrt(ms + EPS) * g


def _layer_spec(stacked, layer):
    tail = stacked.shape[1:]
    return pl.BlockSpec((None,) + tail, lambda *_: (layer,) + (0,) * len(tail),
                        pipeline_mode=pl.Buffered(1))


def _pool_norm(x_ref, before_ref, after_ref, tile, g_mix_ref, hext, *, seq_len):
    rows = x_ref.shape[0]
    tiles_per_seq = seq_len // rows
    kk = tile % tiles_per_seq
    g = g_mix_ref[...]
    hext[POOL_HALO:POOL_HALO + rows, :] = _rms(x_ref[...], g)
    hext[0:POOL_HALO, :] = jnp.where(kk > 0, _rms(before_ref[...], g), 0.0)
    hext[POOL_HALO + rows:, :] = jnp.where(kk < tiles_per_seq - 1, _rms(after_ref[...], g), 0.0)


def _pool_group(gi, x_ref, tile, prm, x1_dst, hext, *, seq_len):
    _, w_ref, b_ref, sc_ref, _ = prm
    rows = x_ref.shape[0]
    gd = w_ref.shape[1]
    kk = tile % (seq_len // rows)
    ext_rows = rows + 2 * POOL_HALO
    edge = V7X_SUBLANES
    t_top = kk * rows + lax.broadcasted_iota(jnp.int32, (edge, 1), 0)
    t_bot = t_top + (rows - edge)
    w = POOL_WINDOWS[gi]
    half = w // 2
    cs = slice(gi * gd, (gi + 1) * gd)

    def count(t):
        return (jnp.minimum(t + half, seq_len) - jnp.maximum(t - half, 0)).astype(F32)

    run, width = hext[:, cs], 1
    while width < half:
        run = run + pltpu.roll(run, ext_rows - width, 0)
        width *= 2
    start = POOL_HALO - half
    wsum = run[start:start + rows] + run[POOL_HALO:POOL_HALO + rows]
    wmean = jnp.concatenate([wsum[:edge] / count(t_top),
                             wsum[edge:rows - edge] * (1.0 / w),
                             wsum[rows - edge:] / count(t_bot)], axis=0)
    pooled = wmean - hext[POOL_HALO:POOL_HALO + rows, cs]
    y = jnp.dot(pooled.astype(BF16), w_ref[gi], preferred_element_type=F32) + b_ref[:, cs]
    x1_dst[:, cs] = x_ref[:, cs] + y * sc_ref[:, cs]


def _attn_prologue(x_ref, o_ref, prm, x1_dst, h_dst):
    wo_ref, g_ffn_ref = prm
    x1_dst[...] = x_ref[...] + jnp.dot(o_ref[...], wo_ref[...], preferred_element_type=F32)
    h_dst[...] = _rms(x1_dst[...], g_ffn_ref[...]).astype(BF16)


def _mix_ffn_kernel(*refs, mode, d_ff, seq_len):
    n_in = {"pool": 5, "attn": 2}[mode]
    n_prm = {"pool": 5, "attn": 2}[mode]
    tile_refs, prm = refs[:n_in], refs[n_in:n_in + n_prm]
    wgu_ref, wd_ref, out_ref = refs[n_in + n_prm:n_in + n_prm + 3]
    h_cur, x1_cur, a_scr = refs[n_in + n_prm + 3:n_in + n_prm + 6]
    i = pl.program_id(0)

    n_chunks = d_ff // FFN_COL_CHUNK
    if mode == "pool":
        x_first, after_first, x_la, before_la, after_la = tile_refs
        h_next, x1_next, hext = refs[-3:]
        g_ffn_ref = prm[-1]
        n_groups = len(POOL_WINDOWS)

        def first():
            _pool_norm(x_first, after_first, after_first, 0, prm[0], hext, seq_len=seq_len)
            for gi in range(n_groups):
                _pool_group(gi, x_first, 0, prm, x1_cur, hext, seq_len=seq_len)
            h_cur[...] = _rms(x1_cur[...], g_ffn_ref[...]).astype(BF16)

        ahead = {0: [functools.partial(_pool_norm, x_la, before_la, after_la, i + 1, prm[0], hext,
                                       seq_len=seq_len)]}
        for gi in range(n_groups):
            ahead.setdefault(2 * gi + 1, []).append(
                functools.partial(_pool_group, gi, x_la, i + 1, prm, x1_next, hext, seq_len=seq_len))

        def finish():
            h_next[...] = _rms(x1_next[...], g_ffn_ref[...]).astype(BF16)
        ahead.setdefault(2 * n_groups, []).append(finish)
        assert max(ahead) < n_chunks

        @pl.when(i == 0)
        def _():
            first()

        @pl.when(i > 0)
        def _():
            h_cur[...] = h_next[...]
            x1_cur[...] = x1_next[...]
    else:
        ahead = {}
        _attn_prologue(*tile_refs, prm, x1_cur, h_cur)

    for c in range(n_chunks):
        for step in ahead.get(c, ()):
            step()
        lo = c * FFN_COL_CHUNK
        gate = jnp.dot(h_cur[...], wgu_ref[:, lo:lo + FFN_COL_CHUNK], preferred_element_type=F32)
        up = jnp.dot(h_cur[...], wgu_ref[:, d_ff + lo:d_ff + lo + FFN_COL_CHUNK], preferred_element_type=F32)
        a_scr[:, lo:lo + FFN_COL_CHUNK] = (jax.nn.silu(gate) * up).astype(BF16)
    out_ref[...] = x1_cur[...] + jnp.dot(a_scr[...], wd_ref[...], preferred_element_type=F32)


def _mix_ffn_layer(mode, x2d, seq_len, mix_args, mix_specs, g_ffn, wgu, wd, layer, o2d=None):
    n, d = x2d.shape
    d_ff = wd.shape[1]
    n_tiles = n // FFN_ROWS
    blk = FFN_ROWS // POOL_HALO
    const = dict(pipeline_mode=pl.Buffered(1))
    first = pl.BlockSpec((FFN_ROWS, d), lambda i: (0, 0), **const)
    look = pl.BlockSpec((FFN_ROWS, d), lambda i: (jnp.minimum(i + 1, n_tiles - 1), 0))
    if mode == "pool":
        tile_specs = [first,
                      pl.BlockSpec((POOL_HALO, d), lambda i: (blk, 0), **const),
                      look,
                      pl.BlockSpec((POOL_HALO, d), lambda i: (jnp.minimum(i + 1, n_tiles - 1) * blk - 1, 0)),
                      pl.BlockSpec((POOL_HALO, d), lambda i: (jnp.minimum((i + 2) * blk, n_tiles * blk - 1), 0))]
        tile_args = [x2d] * 5
        extra_scratch = [pltpu.VMEM((FFN_ROWS, d), BF16), pltpu.VMEM((FFN_ROWS, d), F32),
                         pltpu.VMEM((FFN_ROWS + 2 * POOL_HALO, d), F32)]
    else:
        tile_specs = [pl.BlockSpec((FFN_ROWS, d), lambda i: (i, 0))] * 2
        tile_args = [x2d, o2d]
        extra_scratch = []
    return pl.pallas_call(
        functools.partial(_mix_ffn_kernel, mode=mode, d_ff=d_ff, seq_len=seq_len),
        grid=(n_tiles,),
        in_specs=tile_specs + mix_specs + [_layer_spec(g_ffn, layer), _layer_spec(wgu, layer),
                                           _layer_spec(wd, layer)],
        out_specs=pl.BlockSpec((FFN_ROWS, d), lambda i: (i, 0)),
        out_shape=jax.ShapeDtypeStruct((n, d), F32),
        scratch_shapes=[pltpu.VMEM((FFN_ROWS, d), BF16), pltpu.VMEM((FFN_ROWS, d), F32),
                        pltpu.VMEM((FFN_ROWS, d_ff), BF16)] + extra_scratch,
        compiler_params=pltpu.CompilerParams(
            dimension_semantics=("arbitrary",), vmem_limit_bytes=VMEM_LIMIT),
        name=mode + "_ffn",
    )(*tile_args, *mix_args, g_ffn, wgu, wd)


def _qkv_kernel(x_ref, g_ref, wt_ref, aq_ref, bq_ref, ak_ref, bk_ref,
                qt_ref, k_ref, vt_ref, h_scr, *, n_heads):
    h_scr[...] = _rms(x_ref[...], g_ref[...]).astype(BF16)
    n_groups = wt_ref.shape[0] // QKV_COL_GROUP
    heads_per_group = QKV_COL_GROUP // HEAD_DIM
    half = HEAD_DIM // 2

    def project(gi):
        return lax.dot_general(wt_ref[gi * QKV_COL_GROUP:(gi + 1) * QKV_COL_GROUP, :], h_scr[...],
                               (((1,), (1,)), ((), ())), preferred_element_type=F32)

    def norm_rope(xt, a_ref, b_ref):
        inv = lax.rsqrt(jnp.mean(xt * xt, axis=0, keepdims=True) + EPS)
        swapped = jnp.concatenate([xt[half:], xt[:half]], axis=0)
        return (xt * a_ref[...] + swapped * b_ref[...]) * inv

    nxt = project(0)
    for gi in range(n_groups):
        cur = nxt
        if gi + 1 < n_groups:
            nxt = project(gi + 1)
        for j in range(heads_per_group):
            hd = gi * heads_per_group + j
            xt = cur[j * HEAD_DIM:(j + 1) * HEAD_DIM, :]
            if hd < n_heads:
                qt_ref[hd] = norm_rope(xt, aq_ref, bq_ref).astype(BF16)
            elif hd < n_heads + N_KV_HEADS:
                k_ref[hd - n_heads] = norm_rope(xt, ak_ref, bk_ref).T.astype(BF16)
            else:
                vt_ref[hd - n_heads - N_KV_HEADS] = xt.astype(BF16)


def _rope_layout():
    quarter = ROPE_AXIS_DIM // 2
    idx = jnp.arange(HEAD_DIM)
    blk = idx // quarter
    return jnp.array([0, 2, 1, 3])[blk] * quarter + idx % quarter


def _rope_tables(t):
    pos = jnp.arange(t, dtype=F32)
    row = jnp.floor(pos / GRID_W)
    col = pos - row * GRID_W
    inv_freq = ROPE_THETA ** (-jnp.arange(0, ROPE_AXIS_DIM, 2, dtype=F32) / ROPE_AXIS_DIM)
    ang_r = inv_freq[:, None] * row[None, :]
    ang_c = inv_freq[:, None] * col[None, :]
    ang = jnp.concatenate([ang_r, ang_c, ang_r, ang_c], axis=0)
    sign = jnp.where(jnp.arange(HEAD_DIM) < HEAD_DIM // 2, -1.0, 1.0)
    return jnp.cos(ang), jnp.sin(ang) * sign[:, None]


def _gain_rope_tables(gain, tables):
    cos, sin = tables
    return gain[:, None] * cos, jnp.roll(gain, HEAD_DIM // 2)[:, None] * sin


def _qkv_weight_t(w, n_heads):
    d, e = w.shape
    n_qk = n_heads + N_KV_HEADS
    wh = w.reshape(d, e // HEAD_DIM, HEAD_DIM)
    qk = jnp.take(wh[:, :n_qk], _rope_layout(), axis=2)
    return jnp.concatenate([qk, wh[:, n_qk:]], axis=1).reshape(d, e).T


def _qkv_layer(x, g, layer, wt, attn_layer, q_tabs, k_tabs):
    bsz, t, d = x.shape
    n_heads = d // HEAD_DIM
    tab_spec = pl.BlockSpec((HEAD_DIM, QKV_ROWS), lambda bi, i: (0, i))
    return pl.pallas_call(
        functools.partial(_qkv_kernel, n_heads=n_heads),
        grid=(bsz, t // QKV_ROWS),
        in_specs=[pl.BlockSpec((None, QKV_ROWS, d), lambda bi, i: (bi, i, 0)),
                  _layer_spec(g, layer), _layer_spec(wt, attn_layer),
                  tab_spec, tab_spec, tab_spec, tab_spec],
        out_specs=[pl.BlockSpec((None, n_heads, HEAD_DIM, QKV_ROWS), lambda bi, i: (bi, 0, 0, i)),
                   pl.BlockSpec((None, N_KV_HEADS, QKV_ROWS, HEAD_DIM), lambda bi, i: (bi, 0, i, 0)),
                   pl.BlockSpec((None, N_KV_HEADS, HEAD_DIM, QKV_ROWS), lambda bi, i: (bi, 0, 0, i))],
        out_shape=[jax.ShapeDtypeStruct((bsz, n_heads, HEAD_DIM, t), BF16),
                   jax.ShapeDtypeStruct((bsz, N_KV_HEADS, t, HEAD_DIM), BF16),
                   jax.ShapeDtypeStruct((bsz, N_KV_HEADS, HEAD_DIM, t), BF16)],
        scratch_shapes=[pltpu.VMEM((QKV_ROWS, d), BF16)],
        compiler_params=pltpu.CompilerParams(
            dimension_semantics=("arbitrary", "arbitrary"), vmem_limit_bytes=VMEM_LIMIT),
        name="qkv",
    )(x, g, wt, *q_tabs, *k_tabs)


def _attn_units(k_ref, group):
    n_chunks = k_ref.shape[0] // ATTN_KEY_CHUNK
    return [(hd, c) for hd in range(group) for c in range(n_chunks)], n_chunks


def _attn_scores(qt_ref, k_ref, hd, c):
    return jnp.dot(k_ref[c * ATTN_KEY_CHUNK:(c + 1) * ATTN_KEY_CHUNK, :], qt_ref[hd],
                   preferred_element_type=F32)


def _attn_store(o_ref, hd, acc, denom):
    o_ref[:, hd * HEAD_DIM:(hd + 1) * HEAD_DIM] = (acc / denom).T.astype(BF16)


def _attn_fixed_offset(offset, qt_ref, k_ref, vt_ref, o_ref, group):
    units, n_chunks = _attn_units(k_ref, group)

    def probs(u):
        hd, c = units[u]
        e = jnp.exp2(_attn_scores(qt_ref, k_ref, hd, c) - offset)
        return e.astype(BF16), jnp.sum(e, axis=0, keepdims=True)

    nxt = probs(0)
    for u, (hd, c) in enumerate(units):
        p, esum = nxt
        if u + 1 < len(units):
            nxt = probs(u + 1)
        pv = jnp.dot(vt_ref[:, c * ATTN_KEY_CHUNK:(c + 1) * ATTN_KEY_CHUNK], p,
                     preferred_element_type=F32)
        acc, denom = (pv, esum) if c == 0 else (acc + pv, denom + esum)
        if c == n_chunks - 1:
            _attn_store(o_ref, hd, acc, denom)


def _attn_online(qt_ref, k_ref, vt_ref, o_ref, st_scr, group):
    rows = qt_ref.shape[-1]
    units, n_chunks = _attn_units(k_ref, group)
    n_units = len(units)
    cmax = {}

    def scores(u):
        st_scr[u % ATTN_STAGES] = _attn_scores(qt_ref, k_ref, *units[u])

    def chunk_max(u):
        cmax[u] = jnp.max(st_scr[u % ATTN_STAGES], axis=0, keepdims=True)

    scores(0)
    scores(1)
    chunk_max(0)
    for u, (hd, c) in enumerate(units):
        if u + 2 < n_units:
            scores(u + 2)
        if u + 1 < n_units:
            chunk_max(u + 1)
        if c == 0:
            m = jnp.full((1, rows), -jnp.inf, F32)
            denom = jnp.zeros((1, rows), F32)
            acc = jnp.zeros((HEAD_DIM, rows), F32)
        m_new = jnp.maximum(m, cmax.pop(u))
        alpha = jnp.exp2(m - m_new)
        e = jnp.exp2(st_scr[u % ATTN_STAGES] - m_new)
        denom = alpha * denom + jnp.sum(e, axis=0, keepdims=True)
        acc = alpha * acc + jnp.dot(vt_ref[:, c * ATTN_KEY_CHUNK:(c + 1) * ATTN_KEY_CHUNK],
                                    e.astype(BF16), preferred_element_type=F32)
        m = m_new
        if c == n_chunks - 1:
            _attn_store(o_ref, hd, acc, denom)


def _attn_kernel(bound_ref, qt_ref, k_ref, vt_ref, o_ref, st_scr, *, group):
    bound = bound_ref[0]

    @pl.when(bound <= ATTN_FIXED_OFFSET_MAX_BOUND)
    def _():
        _attn_fixed_offset(bound - ATTN_EXP2_HEADROOM, qt_ref, k_ref, vt_ref, o_ref, group)

    @pl.when(jnp.logical_not(bound <= ATTN_FIXED_OFFSET_MAX_BOUND))
    def _():
        _attn_online(qt_ref, k_ref, vt_ref, o_ref, st_scr, group)


def _score_bound(gq, gk):
    return (HEAD_DIM * BF16_ROUNDING_MARGIN * jnp.max(jnp.abs(gq)) * jnp.max(jnp.abs(gk))).reshape(1)


def _attn_layer(qt, k, vt, bound):
    bsz, n_heads, _, t = qt.shape
    d = n_heads * HEAD_DIM
    group = n_heads // N_KV_HEADS
    gw = group * HEAD_DIM
    o_spec = pl.BlockSpec((None, ATTN_Q_ROWS, gw), lambda bi, kh, i: (bi, i, kh))
    return pl.pallas_call(
        functools.partial(_attn_kernel, group=group),
        grid=(bsz, N_KV_HEADS, t // ATTN_Q_ROWS),
        in_specs=[pl.BlockSpec(memory_space=pltpu.SMEM),
                  pl.BlockSpec((None, group, HEAD_DIM, ATTN_Q_ROWS), lambda bi, kh, i: (bi, kh, 0, i)),
                  pl.BlockSpec((None, None, t, HEAD_DIM), lambda bi, kh, i: (bi, kh, 0, 0)),
                  pl.BlockSpec((None, None, HEAD_DIM, t), lambda bi, kh, i: (bi, kh, 0, 0))],
        out_specs=o_spec,
        out_shape=jax.ShapeDtypeStruct((bsz, t, d), BF16),
        scratch_shapes=[pltpu.VMEM((ATTN_STAGES, ATTN_KEY_CHUNK, ATTN_Q_ROWS), F32)],
        compiler_params=pltpu.CompilerParams(
            dimension_semantics=("arbitrary", "arbitrary", "arbitrary"), vmem_limit_bytes=VMEM_LIMIT),
        name="attn",
    )(bound, qt, k, vt)


def _trunk(x, p):
    bsz, t, d = x.shape
    depth = p["norm_mix"].shape[0]
    x2d = x.reshape(bsz * t, d)
    for i in range(depth):
        j = i // 2
        ffn = (p["norm_ffn"], p["ffn_w_gate_up"], p["ffn_w_down"], i)
        if i % 2 == 0:
            mix = [p["norm_mix"], p["pool_w"], p["pool_b"], p["pool_scale"]]
            specs = [_layer_spec(mix[0], i)] + [_layer_spec(m, j) for m in mix[1:]]
            x2d = _mix_ffn_layer("pool", x2d, t, mix, specs, *ffn)
        else:
            qt, k, vt = _qkv_layer(x2d.reshape(bsz, t, d), p["norm_mix"], i, p["attn_w_qkv_t"], j,
                                   p["q_tables"][j], p["k_tables"][j])
            o2d = _attn_layer(qt, k, vt, p["score_bound"][j]).reshape(bsz * t, d)
            x2d = _mix_ffn_layer("attn", x2d, t, [p["attn_w_o"]], [_layer_spec(p["attn_w_o"], j)],
                                 *ffn, o2d=o2d)
    return x2d.reshape(bsz, t, d)


def kernel(x_prompt, x_sample, norm_mix, norm_ffn, pool_w, pool_b, pool_scale, attn_w_qkv, attn_q_gain, attn_k_gain, attn_w_o, ffn_w_gate_up, ffn_w_down):
    assert x_prompt.shape[1:] == x_sample.shape[1:]
    _, t, d = x_prompt.shape
    n_heads = d // HEAD_DIM
    n_attn = attn_w_qkv.shape[0]
    lanes = _rope_layout()
    gq = attn_q_gain[:, lanes] * (math.log2(math.e) / math.sqrt(HEAD_DIM))
    gk = attn_k_gain[:, lanes]
    tables = _rope_tables(t)
    p = dict(
        norm_mix=norm_mix[:, None, :], norm_ffn=norm_ffn[:, None, :],
        pool_w=pool_w.astype(BF16), pool_b=pool_b.reshape(pool_b.shape[0], 1, d),
        pool_scale=pool_scale[:, None, :],
        attn_w_qkv_t=jax.vmap(lambda w: _qkv_weight_t(w, n_heads))(attn_w_qkv).astype(BF16),
        q_tables=[_gain_rope_tables(gq[j], tables) for j in range(n_attn)],
        k_tables=[_gain_rope_tables(gk[j], tables) for j in range(n_attn)],
        score_bound=[_score_bound(gq[j], gk[j]) for j in range(n_attn)],
        attn_w_o=attn_w_o.astype(BF16),
        ffn_w_gate_up=ffn_w_gate_up.astype(BF16), ffn_w_down=ffn_w_down.astype(BF16),
    )
    return (_trunk(x_prompt, p), _trunk(x_sample, p))
```

```python
import functools
import math

import jax
import jax.numpy as jnp
from jax import lax
from jax.experimental import pallas as pl
from jax.experimental.pallas import tpu as pltpu

F32 = jnp.float32
BF16 = jnp.bfloat16

POOL_WINDOWS = (2, 4, 8, 16)
HEAD_DIM = 128
N_KV_HEADS = 2
ROPE_AXIS_DIM = HEAD_DIM // 2
ROPE_THETA = 10000.0
GRID_W = 64
EPS = 1e-6

V7X_SUBLANES = 8
V7X_VMEM_BYTES = 64 * 1024 * 1024

FFN_ROWS = {"pool": 512, "attn": 1024}
FFN_COL_CHUNK = 256
QKV_ROWS = 1024
QKV_COL_GROUP = 512
ATTN_Q_ROWS = 512
ATTN_KEY_CHUNK = 512
ATTN_EXP2_HEADROOM = 60.0
ATTN_FIXED_OFFSET_MAX_BOUND = 75.0
BF16_ROUNDING_MARGIN = 1.02
ATTN_STAGES = 3
POOL_HALO = V7X_SUBLANES
VMEM_LIMIT = V7X_VMEM_BYTES - 8 * 1024 * 1024


def _rms(x, g):
    ms = jnp.mean(x * x, axis=-1, keepdims=True)
    return x * lax.rsqrt(ms + EPS) * g


def _layer_spec(stacked, layer):
    tail = stacked.shape[1:]
    return pl.BlockSpec((None,) + tail, lambda *_: (layer,) + (0,) * len(tail),
                        pipeline_mode=pl.Buffered(1))


def _pool_norm(x_ref, before_ref, after_ref, tile, g_mix_ref, hext, *, seq_len):
    rows = x_ref.shape[0]
    tiles_per_seq = seq_len // rows
    kk = tile % tiles_per_seq
    g = g_mix_ref[...]
    hext[POOL_HALO:POOL_HALO + rows, :] = _rms(x_ref[...], g)
    hext[0:POOL_HALO, :] = jnp.where(kk > 0, _rms(before_ref[...], g), 0.0)
    hext[POOL_HALO + rows:, :] = jnp.where(kk < tiles_per_seq - 1, _rms(after_ref[...], g), 0.0)


def _pool_group(gi, x_ref, tile, prm, x1_dst, hext, *, seq_len):
    _, w_ref, b_ref, sc_ref, _ = prm
    rows = x_ref.shape[0]
    gd = w_ref.shape[1]
    kk = tile % (seq_len // rows)
    ext_rows = rows + 2 * POOL_HALO
    edge = V7X_SUBLANES
    t_top = kk * rows + lax.broadcasted_iota(jnp.int32, (edge, 1), 0)
    t_bot = t_top + (rows - edge)
    w = POOL_WINDOWS[gi]
    half = w // 2
    cs = slice(gi * gd, (gi + 1) * gd)

    def count(t):
        return (jnp.minimum(t + half, seq_len) - jnp.maximum(t - half, 0)).astype(F32)

    run, width = hext[:, cs], 1
    while width < half:
        run = run + pltpu.roll(run, ext_rows - width, 0)
        width *= 2
    start = POOL_HALO - half
    wsum = run[start:start + rows] + run[POOL_HALO:POOL_HALO + rows]
    wmean = jnp.concatenate([wsum[:edge] / count(t_top),
                             wsum[edge:rows - edge] * (1.0 / w),
                             wsum[rows - edge:] / count(t_bot)], axis=0)
    pooled = wmean - hext[POOL_HALO:POOL_HALO + rows, cs]
    y = jnp.dot(pooled.astype(BF16), w_ref[gi], preferred_element_type=F32) + b_ref[:, cs]
    x1_dst[:, cs] = x_ref[:, cs] + y * sc_ref[:, cs]


def _attn_prologue(x_ref, o_ref, prm, x1_dst, h_dst):
    wo_ref, g_ffn_ref = prm
    x1_dst[...] = x_ref[...] + jnp.dot(o_ref[...], wo_ref[...], preferred_element_type=F32)
    h_dst[...] = _rms(x1_dst[...], g_ffn_ref[...]).astype(BF16)


def _mix_ffn_kernel(*refs, mode, d_ff, seq_len):
    n_in = {"pool": 5, "attn": 2}[mode]
    n_prm = {"pool": 5, "attn": 2}[mode]
    tile_refs, prm = refs[:n_in], refs[n_in:n_in + n_prm]
    wgu_ref, wd_ref, out_ref = refs[n_in + n_prm:n_in + n_prm + 3]
    h_cur, x1_cur, a_scr = refs[n_in + n_prm + 3:n_in + n_prm + 6]
    i = pl.program_id(0)

    n_chunks = d_ff // FFN_COL_CHUNK
    if mode == "pool":
        x_first, after_first, x_la, before_la, after_la = tile_refs
        h_next, x1_next, hext = refs[-3:]
        g_ffn_ref = prm[-1]
        n_groups = len(POOL_WINDOWS)

        def first():
            _pool_norm(x_first, after_first, after_first, 0, prm[0], hext, seq_len=seq_len)
            for gi in range(n_groups):
                _pool_group(gi, x_first, 0, prm, x1_cur, hext, seq_len=seq_len)
            h_cur[...] = _rms(x1_cur[...], g_ffn_ref[...]).astype(BF16)

        ahead = {0: [functools.partial(_pool_norm, x_la, before_la, after_la, i + 1, prm[0], hext,
                                       seq_len=seq_len)]}
        for gi in range(n_groups):
            ahead.setdefault(2 * gi + 1, []).append(
                functools.partial(_pool_group, gi, x_la, i + 1, prm, x1_next, hext, seq_len=seq_len))

        def finish():
            h_next[...] = _rms(x1_next[...], g_ffn_ref[...]).astype(BF16)
        ahead.setdefault(2 * n_groups, []).append(finish)
        assert max(ahead) < n_chunks

        @pl.when(i == 0)
        def _():
            first()

        @pl.when(i > 0)
        def _():
            h_cur[...] = h_next[...]
            x1_cur[...] = x1_next[...]
    else:
        ahead = {}
        _attn_prologue(*tile_refs, prm, x1_cur, h_cur)

    for c in range(n_chunks):
        for step in ahead.get(c, ()):
            step()
        lo = c * FFN_COL_CHUNK
        gate = jnp.dot(h_cur[...], wgu_ref[:, lo:lo + FFN_COL_CHUNK], preferred_element_type=F32)
        up = jnp.dot(h_cur[...], wgu_ref[:, d_ff + lo:d_ff + lo + FFN_COL_CHUNK], preferred_element_type=F32)
        a_scr[:, lo:lo + FFN_COL_CHUNK] = (jax.nn.silu(gate) * up).astype(BF16)
    out_ref[...] = x1_cur[...] + jnp.dot(a_scr[...], wd_ref[...], preferred_element_type=F32)


def _mix_ffn_layer(mode, x2d, seq_len, mix_args, mix_specs, g_ffn, wgu, wd, layer, o2d=None):
    n, d = x2d.shape
    d_ff = wd.shape[1]
    rows = FFN_ROWS[mode]
    n_tiles = n // rows
    blk = rows // POOL_HALO
    const = dict(pipeline_mode=pl.Buffered(1))
    first = pl.BlockSpec((rows, d), lambda i: (0, 0), **const)
    look = pl.BlockSpec((rows, d), lambda i: (jnp.minimum(i + 1, n_tiles - 1), 0))
    if mode == "pool":
        tile_specs = [first,
                      pl.BlockSpec((POOL_HALO, d), lambda i: (blk, 0), **const),
                      look,
                      pl.BlockSpec((POOL_HALO, d), lambda i: (jnp.minimum(i + 1, n_tiles - 1) * blk - 1, 0)),
                      pl.BlockSpec((POOL_HALO, d), lambda i: (jnp.minimum((i + 2) * blk, n_tiles * blk - 1), 0))]
        tile_args = [x2d] * 5
        extra_scratch = [pltpu.VMEM((rows, d), BF16), pltpu.VMEM((rows, d), F32),
                         pltpu.VMEM((rows + 2 * POOL_HALO, d), F32)]
    else:
        tile_specs = [pl.BlockSpec((rows, d), lambda i: (i, 0))] * 2
        tile_args = [x2d, o2d]
        extra_scratch = []
    return pl.pallas_call(
        functools.partial(_mix_ffn_kernel, mode=mode, d_ff=d_ff, seq_len=seq_len),
        grid=(n_tiles,),
        in_specs=tile_specs + mix_specs + [_layer_spec(g_ffn, layer), _layer_spec(wgu, layer),
                                           _layer_spec(wd, layer)],
        out_specs=pl.BlockSpec((rows, d), lambda i: (i, 0)),
        out_shape=jax.ShapeDtypeStruct((n, d), F32),
        scratch_shapes=[pltpu.VMEM((rows, d), BF16), pltpu.VMEM((rows, d), F32),
                        pltpu.VMEM((rows, d_ff), BF16)] + extra_scratch,
        compiler_params=pltpu.CompilerParams(
            dimension_semantics=("arbitrary",), vmem_limit_bytes=VMEM_LIMIT),
        name=mode + "_ffn",
    )(*tile_args, *mix_args, g_ffn, wgu, wd)


def _qkv_kernel(x_ref, g_ref, wt_ref, aq_ref, bq_ref, ak_ref, bk_ref,
                qt_ref, k_ref, vt_ref, h_scr, *, n_heads):
    h_scr[...] = _rms(x_ref[...], g_ref[...]).astype(BF16)
    n_groups = wt_ref.shape[0] // QKV_COL_GROUP
    heads_per_group = QKV_COL_GROUP // HEAD_DIM
    half = HEAD_DIM // 2

    def project(gi):
        return lax.dot_general(wt_ref[gi * QKV_COL_GROUP:(gi + 1) * QKV_COL_GROUP, :], h_scr[...],
                               (((1,), (1,)), ((), ())), preferred_element_type=F32)

    def norm_rope(xt, a_ref, b_ref):
        inv = lax.rsqrt(jnp.mean(xt * xt, axis=0, keepdims=True) + EPS)
        swapped = jnp.concatenate([xt[half:], xt[:half]], axis=0)
        return (xt * a_ref[...] + swapped * b_ref[...]) * inv

    nxt = project(0)
    for gi in range(n_groups):
        cur = nxt
        if gi + 1 < n_groups:
            nxt = project(gi + 1)
        for j in range(heads_per_group):
            hd = gi * heads_per_group + j
            xt = cur[j * HEAD_DIM:(j + 1) * HEAD_DIM, :]
            if hd < n_heads:
                qt_ref[hd] = norm_rope(xt, aq_ref, bq_ref).astype(BF16)
            elif hd < n_heads + N_KV_HEADS:
                k_ref[hd - n_heads] = norm_rope(xt, ak_ref, bk_ref).T.astype(BF16)
            else:
                vt_ref[hd - n_heads - N_KV_HEADS] = xt.astype(BF16)


def _rope_layout():
    quarter = ROPE_AXIS_DIM // 2
    idx = jnp.arange(HEAD_DIM)
    blk = idx // quarter
    return jnp.array([0, 2, 1, 3])[blk] * quarter + idx % quarter


def _rope_tables(t):
    pos = jnp.arange(t, dtype=F32)
    row = jnp.floor(pos / GRID_W)
    col = pos - row * GRID_W
    inv_freq = ROPE_THETA ** (-jnp.arange(0, ROPE_AXIS_DIM, 2, dtype=F32) / ROPE_AXIS_DIM)
    ang_r = inv_freq[:, None] * row[None, :]
    ang_c = inv_freq[:, None] * col[None, :]
    ang = jnp.concatenate([ang_r, ang_c, ang_r, ang_c], axis=0)
    sign = jnp.where(jnp.arange(HEAD_DIM) < HEAD_DIM // 2, -1.0, 1.0)
    return jnp.cos(ang), jnp.sin(ang) * sign[:, None]


def _gain_rope_tables(gain, tables):
    cos, sin = tables
    return gain[:, None] * cos, jnp.roll(gain, HEAD_DIM // 2)[:, None] * sin


def _qkv_weight_t(w, n_heads):
    d, e = w.shape
    n_qk = n_heads + N_KV_HEADS
    wh = w.reshape(d, e // HEAD_DIM, HEAD_DIM)
    qk = jnp.take(wh[:, :n_qk], _rope_layout(), axis=2)
    return jnp.concatenate([qk, wh[:, n_qk:]], axis=1).reshape(d, e).T


def _qkv_layer(x, g, layer, wt, attn_layer, q_tabs, k_tabs):
    bsz, t, d = x.shape
    n_heads = d // HEAD_DIM
    tab_spec = pl.BlockSpec((HEAD_DIM, QKV_ROWS), lambda bi, i: (0, i))
    return pl.pallas_call(
        functools.partial(_qkv_kernel, n_heads=n_heads),
        grid=(bsz, t // QKV_ROWS),
        in_specs=[pl.BlockSpec((None, QKV_ROWS, d), lambda bi, i: (bi, i, 0)),
                  _layer_spec(g, layer), _layer_spec(wt, attn_layer),
                  tab_spec, tab_spec, tab_spec, tab_spec],
        out_specs=[pl.BlockSpec((None, n_heads, HEAD_DIM, QKV_ROWS), lambda bi, i: (bi, 0, 0, i)),
                   pl.BlockSpec((None, N_KV_HEADS, QKV_ROWS, HEAD_DIM), lambda bi, i: (bi, 0, i, 0)),
                   pl.BlockSpec((None, N_KV_HEADS, HEAD_DIM, QKV_ROWS), lambda bi, i: (bi, 0, 0, i))],
        out_shape=[jax.ShapeDtypeStruct((bsz, n_heads, HEAD_DIM, t), BF16),
                   jax.ShapeDtypeStruct((bsz, N_KV_HEADS, t, HEAD_DIM), BF16),
                   jax.ShapeDtypeStruct((bsz, N_KV_HEADS, HEAD_DIM, t), BF16)],
        scratch_shapes=[pltpu.VMEM((QKV_ROWS, d), BF16)],
        compiler_params=pltpu.CompilerParams(
            dimension_semantics=("arbitrary", "arbitrary"), vmem_limit_bytes=VMEM_LIMIT),
        name="qkv",
    )(x, g, wt, *q_tabs, *k_tabs)


def _attn_units(k_ref, group):
    n_chunks = k_ref.shape[0] // ATTN_KEY_CHUNK
    return [(hd, c) for hd in range(group) for c in range(n_chunks)], n_chunks


def _attn_scores(qt_ref, k_ref, hd, c):
    return jnp.dot(k_ref[c * ATTN_KEY_CHUNK:(c + 1) * ATTN_KEY_CHUNK, :], qt_ref[hd],
                   preferred_element_type=F32)


def _attn_store(o_ref, hd, acc, denom):
    o_ref[:, hd * HEAD_DIM:(hd + 1) * HEAD_DIM] = (acc / denom).T.astype(BF16)


def _attn_fixed_offset(offset, qt_ref, k_ref, vt_ref, o_ref, group):
    units, n_chunks = _attn_units(k_ref, group)

    def probs(u):
        hd, c = units[u]
        e = jnp.exp2(_attn_scores(qt_ref, k_ref, hd, c) - offset)
        return e.astype(BF16), jnp.sum(e, axis=0, keepdims=True)

    nxt = probs(0)
    for u, (hd, c) in enumerate(units):
        p, esum = nxt
        if u + 1 < len(units):
            nxt = probs(u + 1)
        pv = jnp.dot(vt_ref[:, c * ATTN_KEY_CHUNK:(c + 1) * ATTN_KEY_CHUNK], p,
                     preferred_element_type=F32)
        acc, denom = (pv, esum) if c == 0 else (acc + pv, denom + esum)
        if c == n_chunks - 1:
            _attn_store(o_ref, hd, acc, denom)


def _attn_online(qt_ref, k_ref, vt_ref, o_ref, st_scr, group):
    rows = qt_ref.shape[-1]
    units, n_chunks = _attn_units(k_ref, group)
    n_units = len(units)
    cmax = {}

    def scores(u):
        st_scr[u % ATTN_STAGES] = _attn_scores(qt_ref, k_ref, *units[u])

    def chunk_max(u):
        cmax[u] = jnp.max(st_scr[u % ATTN_STAGES], axis=0, keepdims=True)

    scores(0)
    scores(1)
    chunk_max(0)
    for u, (hd, c) in enumerate(units):
        if u + 2 < n_units:
            scores(u + 2)
        if u + 1 < n_units:
            chunk_max(u + 1)
        if c == 0:
            m = jnp.full((1, rows), -jnp.inf, F32)
            denom = jnp.zeros((1, rows), F32)
            acc = jnp.zeros((HEAD_DIM, rows), F32)
        m_new = jnp.maximum(m, cmax.pop(u))
        alpha = jnp.exp2(m - m_new)
        e = jnp.exp2(st_scr[u % ATTN_STAGES] - m_new)
        denom = alpha * denom + jnp.sum(e, axis=0, keepdims=True)
        acc = alpha * acc + jnp.dot(vt_ref[:, c * ATTN_KEY_CHUNK:(c + 1) * ATTN_KEY_CHUNK],
                                    e.astype(BF16), preferred_element_type=F32)
        m = m_new
        if c == n_chunks - 1:
            _attn_store(o_ref, hd, acc, denom)


def _attn_kernel(bound_ref, qt_ref, k_ref, vt_ref, o_ref, st_scr, *, group):
    bound = bound_ref[0]

    @pl.when(bound <= ATTN_FIXED_OFFSET_MAX_BOUND)
    def _():
        _attn_fixed_offset(bound - ATTN_EXP2_HEADROOM, qt_ref, k_ref, vt_ref, o_ref, group)

    @pl.when(jnp.logical_not(bound <= ATTN_FIXED_OFFSET_MAX_BOUND))
    def _():
        _attn_online(qt_ref, k_ref, vt_ref, o_ref, st_scr, group)


def _score_bound(gq, gk):
    return (HEAD_DIM * BF16_ROUNDING_MARGIN * jnp.max(jnp.abs(gq)) * jnp.max(jnp.abs(gk))).reshape(1)


def _attn_layer(qt, k, vt, bound):
    bsz, n_heads, _, t = qt.shape
    d = n_heads * HEAD_DIM
    group = n_heads // N_KV_HEADS
    gw = group * HEAD_DIM
    o_spec = pl.BlockSpec((None, ATTN_Q_ROWS, gw), lambda bi, kh, i: (bi, i, kh))
    return pl.pallas_call(
        functools.partial(_attn_kernel, group=group),
        grid=(bsz, N_KV_HEADS, t // ATTN_Q_ROWS),
        in_specs=[pl.BlockSpec(memory_space=pltpu.SMEM),
                  pl.BlockSpec((None, group, HEAD_DIM, ATTN_Q_ROWS), lambda bi, kh, i: (bi, kh, 0, i)),
                  pl.BlockSpec((None, None, t, HEAD_DIM), lambda bi, kh, i: (bi, kh, 0, 0)),
                  pl.BlockSpec((None, None, HEAD_DIM, t), lambda bi, kh, i: (bi, kh, 0, 0))],
        out_specs=o_spec,
        out_shape=jax.ShapeDtypeStruct((bsz, t, d), BF16),
        scratch_shapes=[pltpu.VMEM((ATTN_STAGES, ATTN_KEY_CHUNK, ATTN_Q_ROWS), F32)],
        compiler_params=pltpu.CompilerParams(
            dimension_semantics=("arbitrary", "arbitrary", "arbitrary"), vmem_limit_bytes=VMEM_LIMIT),
        name="attn",
    )(bound, qt, k, vt)


def _trunk(x, p):
    bsz, t, d = x.shape
    depth = p["norm_mix"].shape[0]
    x2d = x.reshape(bsz * t, d)
    for i in range(depth):
        j = i // 2
        ffn = (p["norm_ffn"], p["ffn_w_gate_up"], p["ffn_w_down"], i)
        if i % 2 == 0:
            mix = [p["norm_mix"], p["pool_w"], p["pool_b"], p["pool_scale"]]
            specs = [_layer_spec(mix[0], i)] + [_layer_spec(m, j) for m in mix[1:]]
            x2d = _mix_ffn_layer("pool", x2d, t, mix, specs, *ffn)
        else:
            qt, k, vt = _qkv_layer(x2d.reshape(bsz, t, d), p["norm_mix"], i, p["attn_w_qkv_t"], j,
                                   p["q_tables"][j], p["k_tables"][j])
            o2d = _attn_layer(qt, k, vt, p["score_bound"][j]).reshape(bsz * t, d)
            x2d = _mix_ffn_layer("attn", x2d, t, [p["attn_w_o"]], [_layer_spec(p["attn_w_o"], j)],
                                 *ffn, o2d=o2d)
    return x2d.reshape(bsz, t, d)


def kernel(x_prompt, x_sample, norm_mix, norm_ffn, pool_w, pool_b, pool_scale, attn_w_qkv, attn_q_gain, attn_k_gain, attn_w_o, ffn_w_gate_up, ffn_w_down):
    assert x_prompt.shape[1:] == x_sample.shape[1:]
    _, t, d = x_prompt.shape
    n_heads = d // HEAD_DIM
    n_attn = attn_w_qkv.shape[0]
    lanes = _rope_layout()
    gq = attn_q_gain[:, lanes] * (math.log2(math.e) / math.sqrt(HEAD_DIM))
    gk = attn_k_gain[:, lanes]
    tables = _rope_tables(t)
    p = dict(
        norm_mix=norm_mix[:, None, :], norm_ffn=norm_ffn[:, None, :],
        pool_w=pool_w.astype(BF16), pool_b=pool_b.reshape(pool_b.shape[0], 1, d),
        pool_scale=pool_scale[:, None, :],
        attn_w_qkv_t=jax.vmap(lambda w: _qkv_weight_t(w, n_heads))(attn_w_qkv).astype(BF16),
        q_tables=[_gain_rope_tables(gq[j], tables) for j in range(n_attn)],
        k_tables=[_gain_rope_tables(gk[j], tables) for j in range(n_attn)],
        score_bound=[_score_bound(gq[j], gk[j]) for j in range(n_attn)],
        attn_w_o=attn_w_o.astype(BF16),
        ffn_w_gate_up=ffn_w_gate_up.astype(BF16), ffn_w_down=ffn_w_down.astype(BF16),
    )
    return (_trunk(x_prompt, p), _trunk(x_sample, p))
```

```python
import functools
import math

import jax
import jax.numpy as jnp
from jax import lax
from jax.experimental import pallas as pl
from jax.experimental.pallas import tpu as pltpu

F32 = jnp.float32
BF16 = jnp.bfloat16

POOL_WINDOWS = (2, 4, 8, 16)
HEAD_DIM = 128
N_KV_HEADS = 2
ROPE_AXIS_DIM = HEAD_DIM // 2
ROPE_THETA = 10000.0
GRID_W = 64
EPS = 1e-6

V7X_SUBLANES = 8
V7X_VMEM_BYTES = 64 * 1024 * 1024

FFN_ROWS = {"pool": 512, "attn": 1024}
FFN_COL_CHUNK = 256
QKV_ROWS = 1024
QKV_COL_GROUP = 512
ATTN_Q_ROWS = 512
ATTN_KEY_CHUNK = 1024
ATTN_EXP2_HEADROOM = 60.0
ATTN_FIXED_OFFSET_MAX_BOUND = 75.0
BF16_ROUNDING_MARGIN = 1.02
ATTN_STAGES = 3
POOL_HALO = V7X_SUBLANES
VMEM_LIMIT = V7X_VMEM_BYTES - 8 * 1024 * 1024


def _rms(x, g):
    ms = jnp.mean(x * x, axis=-1, keepdims=True)
    return x * lax.rsqrt(ms + EPS) * g


def _layer_spec(stacked, layer):
    tail = stacked.shape[1:]
    return pl.BlockSpec((None,) + tail, lambda *_: (layer,) + (0,) * len(tail),
                        pipeline_mode=pl.Buffered(1))


def _pool_norm(x_ref, before_ref, after_ref, tile, g_mix_ref, hext, *, seq_len):
    rows = x_ref.shape[0]
    tiles_per_seq = seq_len // rows
    kk = tile % tiles_per_seq
    g = g_mix_ref[...]
    hext[POOL_HALO:POOL_HALO + rows, :] = _rms(x_ref[...], g)
    hext[0:POOL_HALO, :] = jnp.where(kk > 0, _rms(before_ref[...], g), 0.0)
    hext[POOL_HALO + rows:, :] = jnp.where(kk < tiles_per_seq - 1, _rms(after_ref[...], g), 0.0)


def _pool_group(gi, x_ref, tile, prm, x1_dst, hext, *, seq_len):
    _, w_ref, b_ref, sc_ref, _ = prm
    rows = x_ref.shape[0]
    gd = w_ref.shape[1]
    kk = tile % (seq_len // rows)
    ext_rows = rows + 2 * POOL_HALO
    edge = V7X_SUBLANES
    t_top = kk * rows + lax.broadcasted_iota(jnp.int32, (edge, 1), 0)
    t_bot = t_top + (rows - edge)
    w = POOL_WINDOWS[gi]
    half = w // 2
    cs = slice(gi * gd, (gi + 1) * gd)

    def count(t):
        return (jnp.minimum(t + half, seq_len) - jnp.maximum(t - half, 0)).astype(F32)

    run, width = hext[:, cs], 1
    while width < half:
        run = run + pltpu.roll(run, ext_rows - width, 0)
        width *= 2
    start = POOL_HALO - half
    wsum = run[start:start + rows] + run[POOL_HALO:POOL_HALO + rows]
    wmean = jnp.concatenate([wsum[:edge] / count(t_top),
                             wsum[edge:rows - edge] * (1.0 / w),
                             wsum[rows - edge:] / count(t_bot)], axis=0)
    pooled = wmean - hext[POOL_HALO:POOL_HALO + rows, cs]
    y = jnp.dot(pooled.astype(BF16), w_ref[gi], preferred_element_type=F32) + b_ref[:, cs]
    x1_dst[:, cs] = x_ref[:, cs] + y * sc_ref[:, cs]


def _attn_prologue(x_ref, o_ref, prm, x1_dst, h_dst):
    wo_ref, g_ffn_ref = prm
    x1_dst[...] = x_ref[...] + jnp.dot(o_ref[...], wo_ref[...], preferred_element_type=F32)
    h_dst[...] = _rms(x1_dst[...], g_ffn_ref[...]).astype(BF16)


def _mix_ffn_kernel(*refs, mode, d_ff, seq_len):
    n_in = {"pool": 5, "attn": 2}[mode]
    n_prm = {"pool": 5, "attn": 2}[mode]
    tile_refs, prm = refs[:n_in], refs[n_in:n_in + n_prm]
    wgu_ref, wd_ref, out_ref = refs[n_in + n_prm:n_in + n_prm + 3]
    h_scr, a_scr = refs[n_in + n_prm + 3:n_in + n_prm + 5]
    g_ffn_ref = prm[-1]
    i = pl.program_id(0)

    n_chunks = d_ff // FFN_COL_CHUNK
    ahead = {}
    if mode == "pool":
        x_first, after_first, x_la, before_la, after_la = tile_refs
        x1_next, hext = refs[-2:]
        n_groups = len(POOL_WINDOWS)

        def next_h():
            h_scr[...] = _rms(x1_next[...], g_ffn_ref[...]).astype(BF16)

        @pl.when(i == 0)
        def _():
            _pool_norm(x_first, after_first, after_first, 0, prm[0], hext, seq_len=seq_len)
            for gi in range(n_groups):
                _pool_group(gi, x_first, 0, prm, x1_next, hext, seq_len=seq_len)
            next_h()

        out_ref[...] = x1_next[...]
        ahead[0] = [functools.partial(_pool_norm, x_la, before_la, after_la, i + 1, prm[0], hext,
                                      seq_len=seq_len)]
        for gi in range(n_groups):
            ahead[2 * gi + 1] = [functools.partial(_pool_group, gi, x_la, i + 1, prm, x1_next, hext,
                                                   seq_len=seq_len)]
        assert max(ahead) < n_chunks
        ahead[n_chunks] = [next_h]
    else:
        _attn_prologue(*tile_refs, prm, out_ref, h_scr)

    for c in range(n_chunks):
        for step in ahead.get(c, ()):
            step()
        lo = c * FFN_COL_CHUNK
        gate = jnp.dot(h_scr[...], wgu_ref[:, lo:lo + FFN_COL_CHUNK], preferred_element_type=F32)
        up = jnp.dot(h_scr[...], wgu_ref[:, d_ff + lo:d_ff + lo + FFN_COL_CHUNK], preferred_element_type=F32)
        a_scr[:, lo:lo + FFN_COL_CHUNK] = (jax.nn.silu(gate) * up).astype(BF16)
    for step in ahead.get(n_chunks, ()):
        step()
    out_ref[...] = out_ref[...] + jnp.dot(a_scr[...], wd_ref[...], preferred_element_type=F32)


def _mix_ffn_layer(mode, x2d, seq_len, mix_args, mix_specs, g_ffn, wgu, wd, layer, o2d=None):
    n, d = x2d.shape
    d_ff = wd.shape[1]
    rows = FFN_ROWS[mode]
    n_tiles = n // rows
    blk = rows // POOL_HALO
    const = dict(pipeline_mode=pl.Buffered(1))
    first = pl.BlockSpec((rows, d), lambda i: (0, 0), **const)
    look = pl.BlockSpec((rows, d), lambda i: (jnp.minimum(i + 1, n_tiles - 1), 0))
    if mode == "pool":
        tile_specs = [first,
                      pl.BlockSpec((POOL_HALO, d), lambda i: (blk, 0), **const),
                      look,
                      pl.BlockSpec((POOL_HALO, d), lambda i: (jnp.minimum(i + 1, n_tiles - 1) * blk - 1, 0)),
                      pl.BlockSpec((POOL_HALO, d), lambda i: (jnp.minimum((i + 2) * blk, n_tiles * blk - 1), 0))]
        tile_args = [x2d] * 5
        extra_scratch = [pltpu.VMEM((rows, d), F32), pltpu.VMEM((rows + 2 * POOL_HALO, d), F32)]
    else:
        tile_specs = [pl.BlockSpec((rows, d), lambda i: (i, 0))] * 2
        tile_args = [x2d, o2d]
        extra_scratch = []
    return pl.pallas_call(
        functools.partial(_mix_ffn_kernel, mode=mode, d_ff=d_ff, seq_len=seq_len),
        grid=(n_tiles,),
        in_specs=tile_specs + mix_specs + [_layer_spec(g_ffn, layer), _layer_spec(wgu, layer),
                                           _layer_spec(wd, layer)],
        out_specs=pl.BlockSpec((rows, d), lambda i: (i, 0)),
        out_shape=jax.ShapeDtypeStruct((n, d), F32),
        scratch_shapes=[pltpu.VMEM((rows, d), BF16), pltpu.VMEM((rows, d_ff), BF16)] + extra_scratch,
        compiler_params=pltpu.CompilerParams(
            dimension_semantics=("arbitrary",), vmem_limit_bytes=VMEM_LIMIT),
        name=mode + "_ffn",
    )(*tile_args, *mix_args, g_ffn, wgu, wd)


def _qkv_kernel(x_ref, g_ref, wt_ref, aq_ref, bq_ref, ak_ref, bk_ref,
                qt_ref, k_ref, vt_ref, h_scr, *, n_heads):
    h_scr[...] = _rms(x_ref[...], g_ref[...]).astype(BF16)
    n_groups = wt_ref.shape[0] // QKV_COL_GROUP
    heads_per_group = QKV_COL_GROUP // HEAD_DIM
    half = HEAD_DIM // 2

    def project(gi):
        return lax.dot_general(wt_ref[gi * QKV_COL_GROUP:(gi + 1) * QKV_COL_GROUP, :], h_scr[...],
                               (((1,), (1,)), ((), ())), preferred_element_type=F32)

    def norm_rope(xt, a_ref, b_ref):
        inv = lax.rsqrt(jnp.mean(xt * xt, axis=0, keepdims=True) + EPS)
        swapped = jnp.concatenate([xt[half:], xt[:half]], axis=0)
        return (xt * a_ref[...] + swapped * b_ref[...]) * inv

    nxt = project(0)
    for gi in range(n_groups):
        cur = nxt
        if gi + 1 < n_groups:
            nxt = project(gi + 1)
        for j in range(heads_per_group):
            hd = gi * heads_per_group + j
            xt = cur[j * HEAD_DIM:(j + 1) * HEAD_DIM, :]
            if hd < n_heads:
                qt_ref[hd] = norm_rope(xt, aq_ref, bq_ref).astype(BF16)
            elif hd < n_heads + N_KV_HEADS:
                k_ref[hd - n_heads] = norm_rope(xt, ak_ref, bk_ref).T.astype(BF16)
            else:
                vt_ref[hd - n_heads - N_KV_HEADS] = xt.astype(BF16)


def _rope_layout():
    quarter = ROPE_AXIS_DIM // 2
    idx = jnp.arange(HEAD_DIM)
    blk = idx // quarter
    return jnp.array([0, 2, 1, 3])[blk] * quarter + idx % quarter


def _rope_tables(t):
    pos = jnp.arange(t, dtype=F32)
    row = jnp.floor(pos / GRID_W)
    col = pos - row * GRID_W
    inv_freq = ROPE_THETA ** (-jnp.arange(0, ROPE_AXIS_DIM, 2, dtype=F32) / ROPE_AXIS_DIM)
    ang_r = inv_freq[:, None] * row[None, :]
    ang_c = inv_freq[:, None] * col[None, :]
    ang = jnp.concatenate([ang_r, ang_c, ang_r, ang_c], axis=0)
    sign = jnp.where(jnp.arange(HEAD_DIM) < HEAD_DIM // 2, -1.0, 1.0)
    return jnp.cos(ang), jnp.sin(ang) * sign[:, None]


def _gain_rope_tables(gain, tables):
    cos, sin = tables
    return gain[:, None] * cos, jnp.roll(gain, HEAD_DIM // 2)[:, None] * sin


def _qkv_weight_t(w, n_heads):
    d, e = w.shape
    n_qk = n_heads + N_KV_HEADS
    wh = w.reshape(d, e // HEAD_DIM, HEAD_DIM)
    qk = jnp.take(wh[:, :n_qk], _rope_layout(), axis=2)
    return jnp.concatenate([qk, wh[:, n_qk:]], axis=1).reshape(d, e).T


def _qkv_layer(x, g, layer, wt, attn_layer, q_tabs, k_tabs):
    bsz, t, d = x.shape
    n_heads = d // HEAD_DIM
    tab_spec = pl.BlockSpec((HEAD_DIM, QKV_ROWS), lambda bi, i: (0, i))
    return pl.pallas_call(
        functools.partial(_qkv_kernel, n_heads=n_heads),
        grid=(bsz, t // QKV_ROWS),
        in_specs=[pl.BlockSpec((None, QKV_ROWS, d), lambda bi, i: (bi, i, 0)),
                  _layer_spec(g, layer), _layer_spec(wt, attn_layer),
                  tab_spec, tab_spec, tab_spec, tab_spec],
        out_specs=[pl.BlockSpec((None, n_heads, HEAD_DIM, QKV_ROWS), lambda bi, i: (bi, 0, 0, i)),
                   pl.BlockSpec((None, N_KV_HEADS, QKV_ROWS, HEAD_DIM), lambda bi, i: (bi, 0, i, 0)),
                   pl.BlockSpec((None, N_KV_HEADS, HEAD_DIM, QKV_ROWS), lambda bi, i: (bi, 0, 0, i))],
        out_shape=[jax.ShapeDtypeStruct((bsz, n_heads, HEAD_DIM, t), BF16),
                   jax.ShapeDtypeStruct((bsz, N_KV_HEADS, t, HEAD_DIM), BF16),
                   jax.ShapeDtypeStruct((bsz, N_KV_HEADS, HEAD_DIM, t), BF16)],
        scratch_shapes=[pltpu.VMEM((QKV_ROWS, d), BF16)],
        compiler_params=pltpu.CompilerParams(
            dimension_semantics=("arbitrary", "arbitrary"), vmem_limit_bytes=VMEM_LIMIT),
        name="qkv",
    )(x, g, wt, *q_tabs, *k_tabs)


def _attn_units(k_ref, group):
    n_chunks = k_ref.shape[0] // ATTN_KEY_CHUNK
    return [(hd, c) for hd in range(group) for c in range(n_chunks)], n_chunks


def _attn_scores(qt_ref, k_ref, hd, c):
    return jnp.dot(k_ref[c * ATTN_KEY_CHUNK:(c + 1) * ATTN_KEY_CHUNK, :], qt_ref[hd],
                   preferred_element_type=F32)


def _attn_store(o_ref, hd, acc, denom):
    o_ref[:, hd * HEAD_DIM:(hd + 1) * HEAD_DIM] = (acc / denom).T.astype(BF16)


def _attn_fixed_offset(offset, qt_ref, k_ref, vt_ref, o_ref, group):
    units, n_chunks = _attn_units(k_ref, group)

    def probs(u):
        hd, c = units[u]
        e = jnp.exp2(_attn_scores(qt_ref, k_ref, hd, c) - offset)
        return e.astype(BF16), jnp.sum(e, axis=0, keepdims=True)

    nxt = probs(0)
    for u, (hd, c) in enumerate(units):
        p, esum = nxt
        if u + 1 < len(units):
            nxt = probs(u + 1)
        pv = jnp.dot(vt_ref[:, c * ATTN_KEY_CHUNK:(c + 1) * ATTN_KEY_CHUNK], p,
                     preferred_element_type=F32)
        acc, denom = (pv, esum) if c == 0 else (acc + pv, denom + esum)
        if c == n_chunks - 1:
            _attn_store(o_ref, hd, acc, denom)


def _attn_online(qt_ref, k_ref, vt_ref, o_ref, st_scr, group):
    rows = qt_ref.shape[-1]
    units, n_chunks = _attn_units(k_ref, group)
    n_units = len(units)
    cmax = {}

    def scores(u):
        st_scr[u % ATTN_STAGES] = _attn_scores(qt_ref, k_ref, *units[u])

    def chunk_max(u):
        cmax[u] = jnp.max(st_scr[u % ATTN_STAGES], axis=0, keepdims=True)

    scores(0)
    scores(1)
    chunk_max(0)
    for u, (hd, c) in enumerate(units):
        if u + 2 < n_units:
            scores(u + 2)
        if u + 1 < n_units:
            chunk_max(u + 1)
        if c == 0:
            m = jnp.full((1, rows), -jnp.inf, F32)
            denom = jnp.zeros((1, rows), F32)
            acc = jnp.zeros((HEAD_DIM, rows), F32)
        m_new = jnp.maximum(m, cmax.pop(u))
        alpha = jnp.exp2(m - m_new)
        e = jnp.exp2(st_scr[u % ATTN_STAGES] - m_new)
        denom = alpha * denom + jnp.sum(e, axis=0, keepdims=True)
        acc = alpha * acc + jnp.dot(vt_ref[:, c * ATTN_KEY_CHUNK:(c + 1) * ATTN_KEY_CHUNK],
                                    e.astype(BF16), preferred_element_type=F32)
        m = m_new
        if c == n_chunks - 1:
            _attn_store(o_ref, hd, acc, denom)


def _attn_kernel(bound_ref, qt_ref, k_ref, vt_ref, o_ref, st_scr, *, group):
    bound = bound_ref[0]

    @pl.when(bound <= ATTN_FIXED_OFFSET_MAX_BOUND)
    def _():
        _attn_fixed_offset(bound - ATTN_EXP2_HEADROOM, qt_ref, k_ref, vt_ref, o_ref, group)

    @pl.when(jnp.logical_not(bound <= ATTN_FIXED_OFFSET_MAX_BOUND))
    def _():
        _attn_online(qt_ref, k_ref, vt_ref, o_ref, st_scr, group)


def _score_bound(gq, gk):
    return (HEAD_DIM * BF16_ROUNDING_MARGIN * jnp.max(jnp.abs(gq)) * jnp.max(jnp.abs(gk))).reshape(1)


def _attn_layer(qt, k, vt, bound):
    bsz, n_heads, _, t = qt.shape
    d = n_heads * HEAD_DIM
    group = n_heads // N_KV_HEADS
    gw = group * HEAD_DIM
    o_spec = pl.BlockSpec((None, ATTN_Q_ROWS, gw), lambda bi, kh, i: (bi, i, kh))
    return pl.pallas_call(
        functools.partial(_attn_kernel, group=group),
        grid=(bsz, N_KV_HEADS, t // ATTN_Q_ROWS),
        in_specs=[pl.BlockSpec(memory_space=pltpu.SMEM),
                  pl.BlockSpec((None, group, HEAD_DIM, ATTN_Q_ROWS), lambda bi, kh, i: (bi, kh, 0, i)),
                  pl.BlockSpec((None, None, t, HEAD_DIM), lambda bi, kh, i: (bi, kh, 0, 0)),
                  pl.BlockSpec((None, None, HEAD_DIM, t), lambda bi, kh, i: (bi, kh, 0, 0))],
        out_specs=o_spec,
        out_shape=jax.ShapeDtypeStruct((bsz, t, d), BF16),
        scratch_shapes=[pltpu.VMEM((ATTN_STAGES, ATTN_KEY_CHUNK, ATTN_Q_ROWS), F32)],
        compiler_params=pltpu.CompilerParams(
            dimension_semantics=("arbitrary", "arbitrary", "arbitrary"), vmem_limit_bytes=VMEM_LIMIT),
        name="attn",
    )(bound, qt, k, vt)


def _trunk(x, p):
    bsz, t, d = x.shape
    depth = p["norm_mix"].shape[0]
    x2d = x.reshape(bsz * t, d)
    for i in range(depth):
        j = i // 2
        ffn = (p["norm_ffn"], p["ffn_w_gate_up"], p["ffn_w_down"], i)
        if i % 2 == 0:
            mix = [p["norm_mix"], p["pool_w"], p["pool_b"], p["pool_scale"]]
            specs = [_layer_spec(mix[0], i)] + [_layer_spec(m, j) for m in mix[1:]]
            x2d = _mix_ffn_layer("pool", x2d, t, mix, specs, *ffn)
        else:
            qt, k, vt = _qkv_layer(x2d.reshape(bsz, t, d), p["norm_mix"], i, p["attn_w_qkv_t"], j,
                                   p["q_tables"][j], p["k_tables"][j])
            o2d = _attn_layer(qt, k, vt, p["score_bound"][j]).reshape(bsz * t, d)
            x2d = _mix_ffn_layer("attn", x2d, t, [p["attn_w_o"]], [_layer_spec(p["attn_w_o"], j)],
                                 *ffn, o2d=o2d)
    return x2d.reshape(bsz, t, d)


def kernel(x_prompt, x_sample, norm_mix, norm_ffn, pool_w, pool_b, pool_scale, attn_w_qkv, attn_q_gain, attn_k_gain, attn_w_o, ffn_w_gate_up, ffn_w_down):
    assert x_prompt.shape[1:] == x_sample.shape[1:]
    _, t, d = x_prompt.shape
    n_heads = d // HEAD_DIM
    n_attn = attn_w_qkv.shape[0]
    lanes = _rope_layout()
    gq = attn_q_gain[:, lanes] * (math.log2(math.e) / math.sqrt(HEAD_DIM))
    gk = attn_k_gain[:, lanes]
    tables = _rope_tables(t)
    p = dict(
        norm_mix=norm_mix[:, None, :], norm_ffn=norm_ffn[:, None, :],
        pool_w=pool_w.astype(BF16), pool_b=pool_b.reshape(pool_b.shape[0], 1, d),
        pool_scale=pool_scale[:, None, :],
        attn_w_qkv_t=jax.vmap(lambda w: _qkv_weight_t(w, n_heads))(attn_w_qkv).astype(BF16),
        q_tables=[_gain_rope_tables(gq[j], tables) for j in range(n_attn)],
        k_tables=[_gain_rope_tables(gk[j], tables) for j in range(n_attn)],
        score_bound=[_score_bound(gq[j], gk[j]) for j in range(n_attn)],
        attn_w_o=attn_w_o.astype(BF16),
        ffn_w_gate_up=ffn_w_gate_up.astype(BF16), ffn_w_down=ffn_w_down.astype(BF16),
    )
    return (_trunk(x_prompt, p), _trunk(x_sample, p))
```

```python
import functools
import math

import jax
import jax.numpy as jnp
from jax import lax
from jax.experimental import pallas as pl
from jax.experimental.pallas import tpu as pltpu

F32 = jnp.float32
BF16 = jnp.bfloat16

POOL_WINDOWS = (2, 4, 8, 16)
HEAD_DIM = 128
N_KV_HEADS = 2
ROPE_AXIS_DIM = HEAD_DIM // 2
ROPE_THETA = 10000.0
GRID_W = 64
EPS = 1e-6

V7X_SUBLANES = 8
V7X_VMEM_BYTES = 64 * 1024 * 1024

FFN_ROWS = {"pool": 512, "attn": 1024}
FFN_COL_CHUNK = 256
QKV_ROWS = 1024
QKV_COL_GROUP = 512
ATTN_Q_ROWS = 512
ATTN_KEY_CHUNK = 4096
ATTN_EXP2_HEADROOM = 60.0
ATTN_FIXED_OFFSET_MAX_BOUND = 75.0
BF16_ROUNDING_MARGIN = 1.02
ATTN_STAGES = 3
POOL_HALO = V7X_SUBLANES
VMEM_LIMIT = V7X_VMEM_BYTES - 8 * 1024 * 1024


def _rms(x, g):
    ms = jnp.mean(x * x, axis=-1, keepdims=True)
    return x * lax.rsqrt(ms + EPS) * g


def _layer_spec(stacked, layer):
    tail = stacked.shape[1:]
    return pl.BlockSpec((None,) + tail, lambda *_: (layer,) + (0,) * len(tail),
                        pipeline_mode=pl.Buffered(1))


def _pool_norm(x_ref, before_ref, after_ref, tile, g_mix_ref, hext, *, seq_len):
    rows = x_ref.shape[0]
    tiles_per_seq = seq_len // rows
    kk = tile % tiles_per_seq
    g = g_mix_ref[...]
    hext[POOL_HALO:POOL_HALO + rows, :] = _rms(x_ref[...], g)
    hext[0:POOL_HALO, :] = jnp.where(kk > 0, _rms(before_ref[...], g), 0.0)
    hext[POOL_HALO + rows:, :] = jnp.where(kk < tiles_per_seq - 1, _rms(after_ref[...], g), 0.0)


def _pool_group(gi, x_ref, tile, prm, x1_dst, hext, *, seq_len):
    _, w_ref, b_ref, sc_ref, _ = prm
    rows = x_ref.shape[0]
    gd = w_ref.shape[1]
    kk = tile % (seq_len // rows)
    ext_rows = rows + 2 * POOL_HALO
    edge = V7X_SUBLANES
    t_top = kk * rows + lax.broadcasted_iota(jnp.int32, (edge, 1), 0)
    t_bot = t_top + (rows - edge)
    w = POOL_WINDOWS[gi]
    half = w // 2
    cs = slice(gi * gd, (gi + 1) * gd)

    def count(t):
        return (jnp.minimum(t + half, seq_len) - jnp.maximum(t - half, 0)).astype(F32)

    run, width = hext[:, cs], 1
    while width < half:
        run = run + pltpu.roll(run, ext_rows - width, 0)
        width *= 2
    start = POOL_HALO - half
    wsum = run[start:start + rows] + run[POOL_HALO:POOL_HALO + rows]
    wmean = jnp.concatenate([wsum[:edge] / count(t_top),
                             wsum[edge:rows - edge] * (1.0 / w),
                             wsum[rows - edge:] / count(t_bot)], axis=0)
    pooled = wmean - hext[POOL_HALO:POOL_HALO + rows, cs]
    y = jnp.dot(pooled.astype(BF16), w_ref[gi], preferred_element_type=F32) + b_ref[:, cs]
    x1_dst[:, cs] = x_ref[:, cs] + y * sc_ref[:, cs]


def _attn_prologue(x_ref, o_ref, prm, x1_dst, h_dst):
    wo_ref, g_ffn_ref = prm
    x1_dst[...] = x_ref[...] + jnp.dot(o_ref[...], wo_ref[...], preferred_element_type=F32)
    h_dst[...] = _rms(x1_dst[...], g_ffn_ref[...]).astype(BF16)


def _mix_ffn_kernel(*refs, mode, d_ff, seq_len):
    n_in = {"pool": 5, "attn": 2}[mode]
    n_prm = {"pool": 5, "attn": 2}[mode]
    tile_refs, prm = refs[:n_in], refs[n_in:n_in + n_prm]
    wgu_ref, wd_ref, out_ref = refs[n_in + n_prm:n_in + n_prm + 3]
    h_scr, a_scr = refs[n_in + n_prm + 3:n_in + n_prm + 5]
    g_ffn_ref = prm[-1]
    i = pl.program_id(0)

    n_chunks = d_ff // FFN_COL_CHUNK
    ahead = {}
    if mode == "pool":
        x_first, after_first, x_la, before_la, after_la = tile_refs
        x1_next, hext = refs[-2:]
        n_groups = len(POOL_WINDOWS)

        def next_h():
            h_scr[...] = _rms(x1_next[...], g_ffn_ref[...]).astype(BF16)

        @pl.when(i == 0)
        def _():
            _pool_norm(x_first, after_first, after_first, 0, prm[0], hext, seq_len=seq_len)
            for gi in range(n_groups):
                _pool_group(gi, x_first, 0, prm, x1_next, hext, seq_len=seq_len)
            next_h()

        out_ref[...] = x1_next[...]
        ahead[0] = [functools.partial(_pool_norm, x_la, before_la, after_la, i + 1, prm[0], hext,
                                      seq_len=seq_len)]
        for gi in range(n_groups):
            ahead[2 * gi + 1] = [functools.partial(_pool_group, gi, x_la, i + 1, prm, x1_next, hext,
                                                   seq_len=seq_len)]
        assert max(ahead) < n_chunks
        ahead[n_chunks] = [next_h]
    else:
        _attn_prologue(*tile_refs, prm, out_ref, h_scr)

    for c in range(n_chunks):
        for step in ahead.get(c, ()):
            step()
        lo = c * FFN_COL_CHUNK
        gate = jnp.dot(h_scr[...], wgu_ref[:, lo:lo + FFN_COL_CHUNK], preferred_element_type=F32)
        up = jnp.dot(h_scr[...], wgu_ref[:, d_ff + lo:d_ff + lo + FFN_COL_CHUNK], preferred_element_type=F32)
        a_scr[:, lo:lo + FFN_COL_CHUNK] = (jax.nn.silu(gate) * up).astype(BF16)
    for step in ahead.get(n_chunks, ()):
        step()
    out_ref[...] = out_ref[...] + jnp.dot(a_scr[...], wd_ref[...], preferred_element_type=F32)


def _mix_ffn_layer(mode, x2d, seq_len, mix_args, mix_specs, g_ffn, wgu, wd, layer, o2d=None):
    n, d = x2d.shape
    d_ff = wd.shape[1]
    rows = FFN_ROWS[mode]
    n_tiles = n // rows
    blk = rows // POOL_HALO
    const = dict(pipeline_mode=pl.Buffered(1))
    first = pl.BlockSpec((rows, d), lambda i: (0, 0), **const)
    look = pl.BlockSpec((rows, d), lambda i: (jnp.minimum(i + 1, n_tiles - 1), 0))
    if mode == "pool":
        tile_specs = [first,
                      pl.BlockSpec((POOL_HALO, d), lambda i: (blk, 0), **const),
                      look,
                      pl.BlockSpec((POOL_HALO, d), lambda i: (jnp.minimum(i + 1, n_tiles - 1) * blk - 1, 0)),
                      pl.BlockSpec((POOL_HALO, d), lambda i: (jnp.minimum((i + 2) * blk, n_tiles * blk - 1), 0))]
        tile_args = [x2d] * 5
        extra_scratch = [pltpu.VMEM((rows, d), F32), pltpu.VMEM((rows + 2 * POOL_HALO, d), F32)]
    else:
        tile_specs = [pl.BlockSpec((rows, d), lambda i: (i, 0))] * 2
        tile_args = [x2d, o2d]
        extra_scratch = []
    return pl.pallas_call(
        functools.partial(_mix_ffn_kernel, mode=mode, d_ff=d_ff, seq_len=seq_len),
        grid=(n_tiles,),
        in_specs=tile_specs + mix_specs + [_layer_spec(g_ffn, layer), _layer_spec(wgu, layer),
                                           _layer_spec(wd, layer)],
        out_specs=pl.BlockSpec((rows, d), lambda i: (i, 0)),
        out_shape=jax.ShapeDtypeStruct((n, d), F32),
        scratch_shapes=[pltpu.VMEM((rows, d), BF16), pltpu.VMEM((rows, d_ff), BF16)] + extra_scratch,
        compiler_params=pltpu.CompilerParams(
            dimension_semantics=("arbitrary",), vmem_limit_bytes=VMEM_LIMIT),
        name=mode + "_ffn",
    )(*tile_args, *mix_args, g_ffn, wgu, wd)


def _qkv_kernel(x_ref, g_ref, wt_ref, aq_ref, bq_ref, ak_ref, bk_ref,
                qt_ref, k_ref, vt_ref, h_scr, *, n_heads):
    h_scr[...] = _rms(x_ref[...], g_ref[...]).astype(BF16)
    n_groups = wt_ref.shape[0] // QKV_COL_GROUP
    heads_per_group = QKV_COL_GROUP // HEAD_DIM
    half = HEAD_DIM // 2

    def project(gi):
        return lax.dot_general(wt_ref[gi * QKV_COL_GROUP:(gi + 1) * QKV_COL_GROUP, :], h_scr[...],
                               (((1,), (1,)), ((), ())), preferred_element_type=F32)

    def norm_rope(xt, a_ref, b_ref):
        inv = lax.rsqrt(jnp.mean(xt * xt, axis=0, keepdims=True) + EPS)
        swapped = jnp.concatenate([xt[half:], xt[:half]], axis=0)
        return (xt * a_ref[...] + swapped * b_ref[...]) * inv

    nxt = project(0)
    for gi in range(n_groups):
        cur = nxt
        if gi + 1 < n_groups:
            nxt = project(gi + 1)
        for j in range(heads_per_group):
            hd = gi * heads_per_group + j
            xt = cur[j * HEAD_DIM:(j + 1) * HEAD_DIM, :]
            if hd < n_heads:
                qt_ref[hd] = norm_rope(xt, aq_ref, bq_ref).astype(BF16)
            elif hd < n_heads + N_KV_HEADS:
                k_ref[hd - n_heads] = norm_rope(xt, ak_ref, bk_ref).T.astype(BF16)
            else:
                vt_ref[hd - n_heads - N_KV_HEADS] = xt.astype(BF16)


def _rope_layout():
    quarter = ROPE_AXIS_DIM // 2
    idx = jnp.arange(HEAD_DIM)
    blk = idx // quarter
    return jnp.array([0, 2, 1, 3])[blk] * quarter + idx % quarter


def _rope_tables(t):
    pos = jnp.arange(t, dtype=F32)
    row = jnp.floor(pos / GRID_W)
    col = pos - row * GRID_W
    inv_freq = ROPE_THETA ** (-jnp.arange(0, ROPE_AXIS_DIM, 2, dtype=F32) / ROPE_AXIS_DIM)
    ang_r = inv_freq[:, None] * row[None, :]
    ang_c = inv_freq[:, None] * col[None, :]
    ang = jnp.concatenate([ang_r, ang_c, ang_r, ang_c], axis=0)
    sign = jnp.where(jnp.arange(HEAD_DIM) < HEAD_DIM // 2, -1.0, 1.0)
    return jnp.cos(ang), jnp.sin(ang) * sign[:, None]


def _gain_rope_tables(gain, tables):
    cos, sin = tables
    return gain[:, None] * cos, jnp.roll(gain, HEAD_DIM // 2)[:, None] * sin


def _qkv_weight_t(w, n_heads):
    d, e = w.shape
    n_qk = n_heads + N_KV_HEADS
    wh = w.reshape(d, e // HEAD_DIM, HEAD_DIM)
    qk = jnp.take(wh[:, :n_qk], _rope_layout(), axis=2)
    return jnp.concatenate([qk, wh[:, n_qk:]], axis=1).reshape(d, e).T


def _qkv_layer(x, g, layer, wt, attn_layer, q_tabs, k_tabs):
    bsz, t, d = x.shape
    n_heads = d // HEAD_DIM
    tab_spec = pl.BlockSpec((HEAD_DIM, QKV_ROWS), lambda bi, i: (0, i))
    return pl.pallas_call(
        functools.partial(_qkv_kernel, n_heads=n_heads),
        grid=(bsz, t // QKV_ROWS),
        in_specs=[pl.BlockSpec((None, QKV_ROWS, d), lambda bi, i: (bi, i, 0)),
                  _layer_spec(g, layer), _layer_spec(wt, attn_layer),
                  tab_spec, tab_spec, tab_spec, tab_spec],
        out_specs=[pl.BlockSpec((None, n_heads, HEAD_DIM, QKV_ROWS), lambda bi, i: (bi, 0, 0, i)),
                   pl.BlockSpec((None, N_KV_HEADS, QKV_ROWS, HEAD_DIM), lambda bi, i: (bi, 0, i, 0)),
                   pl.BlockSpec((None, N_KV_HEADS, HEAD_DIM, QKV_ROWS), lambda bi, i: (bi, 0, 0, i))],
        out_shape=[jax.ShapeDtypeStruct((bsz, n_heads, HEAD_DIM, t), BF16),
                   jax.ShapeDtypeStruct((bsz, N_KV_HEADS, t, HEAD_DIM), BF16),
                   jax.ShapeDtypeStruct((bsz, N_KV_HEADS, HEAD_DIM, t), BF16)],
        scratch_shapes=[pltpu.VMEM((QKV_ROWS, d), BF16)],
        compiler_params=pltpu.CompilerParams(
            dimension_semantics=("arbitrary", "arbitrary"), vmem_limit_bytes=VMEM_LIMIT),
        name="qkv",
    )(x, g, wt, *q_tabs, *k_tabs)


def _key_chunk(t):
    return min(ATTN_KEY_CHUNK, t)


def _attn_units(k_ref, group):
    n_chunks = k_ref.shape[0] // _key_chunk(k_ref.shape[0])
    return [(hd, c) for hd in range(group) for c in range(n_chunks)], n_chunks


def _chunk(ref, c, axis):
    size = _key_chunk(ref.shape[axis])
    return ref[c * size:(c + 1) * size, :] if axis == 0 else ref[:, c * size:(c + 1) * size]


def _attn_scores(qt_ref, k_ref, hd, c):
    return jnp.dot(_chunk(k_ref, c, 0), qt_ref[hd], preferred_element_type=F32)


def _attn_store(o_ref, hd, acc, denom):
    o_ref[:, hd * HEAD_DIM:(hd + 1) * HEAD_DIM] = (acc / denom).T.astype(BF16)


def _attn_fixed_offset(offset, qt_ref, k_ref, vt_ref, o_ref, group):
    units, n_chunks = _attn_units(k_ref, group)

    def probs(u):
        hd, c = units[u]
        e = jnp.exp2(_attn_scores(qt_ref, k_ref, hd, c) - offset)
        return e.astype(BF16), jnp.sum(e, axis=0, keepdims=True)

    nxt = probs(0)
    for u, (hd, c) in enumerate(units):
        p, esum = nxt
        if u + 1 < len(units):
            nxt = probs(u + 1)
        pv = jnp.dot(_chunk(vt_ref, c, 1), p, preferred_element_type=F32)
        acc, denom = (pv, esum) if c == 0 else (acc + pv, denom + esum)
        if c == n_chunks - 1:
            _attn_store(o_ref, hd, acc, denom)


def _attn_online(qt_ref, k_ref, vt_ref, o_ref, st_scr, group):
    rows = qt_ref.shape[-1]
    units, n_chunks = _attn_units(k_ref, group)
    n_units = len(units)
    cmax = {}

    def scores(u):
        st_scr[u % ATTN_STAGES] = _attn_scores(qt_ref, k_ref, *units[u])

    def chunk_max(u):
        cmax[u] = jnp.max(st_scr[u % ATTN_STAGES], axis=0, keepdims=True)

    scores(0)
    scores(1)
    chunk_max(0)
    for u, (hd, c) in enumerate(units):
        if u + 2 < n_units:
            scores(u + 2)
        if u + 1 < n_units:
            chunk_max(u + 1)
        if c == 0:
            m = jnp.full((1, rows), -jnp.inf, F32)
            denom = jnp.zeros((1, rows), F32)
            acc = jnp.zeros((HEAD_DIM, rows), F32)
        m_new = jnp.maximum(m, cmax.pop(u))
        alpha = jnp.exp2(m - m_new)
        e = jnp.exp2(st_scr[u % ATTN_STAGES] - m_new)
        denom = alpha * denom + jnp.sum(e, axis=0, keepdims=True)
        acc = alpha * acc + jnp.dot(_chunk(vt_ref, c, 1), e.astype(BF16), preferred_element_type=F32)
        m = m_new
        if c == n_chunks - 1:
            _attn_store(o_ref, hd, acc, denom)


def _attn_kernel(bound_ref, qt_ref, k_ref, vt_ref, o_ref, st_scr, *, group):
    bound = bound_ref[0]

    @pl.when(bound <= ATTN_FIXED_OFFSET_MAX_BOUND)
    def _():
        _attn_fixed_offset(bound - ATTN_EXP2_HEADROOM, qt_ref, k_ref, vt_ref, o_ref, group)

    @pl.when(jnp.logical_not(bound <= ATTN_FIXED_OFFSET_MAX_BOUND))
    def _():
        _attn_online(qt_ref, k_ref, vt_ref, o_ref, st_scr, group)


def _score_bound(gq, gk):
    return (HEAD_DIM * BF16_ROUNDING_MARGIN * jnp.max(jnp.abs(gq)) * jnp.max(jnp.abs(gk))).reshape(1)


def _attn_layer(qt, k, vt, bound):
    bsz, n_heads, _, t = qt.shape
    d = n_heads * HEAD_DIM
    group = n_heads // N_KV_HEADS
    gw = group * HEAD_DIM
    o_spec = pl.BlockSpec((None, ATTN_Q_ROWS, gw), lambda bi, kh, i: (bi, i, kh))
    return pl.pallas_call(
        functools.partial(_attn_kernel, group=group),
        grid=(bsz, N_KV_HEADS, t // ATTN_Q_ROWS),
        in_specs=[pl.BlockSpec(memory_space=pltpu.SMEM),
                  pl.BlockSpec((None, group, HEAD_DIM, ATTN_Q_ROWS), lambda bi, kh, i: (bi, kh, 0, i)),
                  pl.BlockSpec((None, None, t, HEAD_DIM), lambda bi, kh, i: (bi, kh, 0, 0)),
                  pl.BlockSpec((None, None, HEAD_DIM, t), lambda bi, kh, i: (bi, kh, 0, 0))],
        out_specs=o_spec,
        out_shape=jax.ShapeDtypeStruct((bsz, t, d), BF16),
        scratch_shapes=[pltpu.VMEM((ATTN_STAGES, _key_chunk(t), ATTN_Q_ROWS), F32)],
        compiler_params=pltpu.CompilerParams(
            dimension_semantics=("arbitrary", "arbitrary", "arbitrary"), vmem_limit_bytes=VMEM_LIMIT),
        name="attn",
    )(bound, qt, k, vt)


def _trunk(x, p):
    bsz, t, d = x.shape
    depth = p["norm_mix"].shape[0]
    x2d = x.reshape(bsz * t, d)
    for i in range(depth):
        j = i // 2
        ffn = (p["norm_ffn"], p["ffn_w_gate_up"], p["ffn_w_down"], i)
        if i % 2 == 0:
            mix = [p["norm_mix"], p["pool_w"], p["pool_b"], p["pool_scale"]]
            specs = [_layer_spec(mix[0], i)] + [_layer_spec(m, j) for m in mix[1:]]
            x2d = _mix_ffn_layer("pool", x2d, t, mix, specs, *ffn)
        else:
            qt, k, vt = _qkv_layer(x2d.reshape(bsz, t, d), p["norm_mix"], i, p["attn_w_qkv_t"], j,
                                   p["q_tables"][j], p["k_tables"][j])
            o2d = _attn_layer(qt, k, vt, p["score_bound"][j]).reshape(bsz * t, d)
            x2d = _mix_ffn_layer("attn", x2d, t, [p["attn_w_o"]], [_layer_spec(p["attn_w_o"], j)],
                                 *ffn, o2d=o2d)
    return x2d.reshape(bsz, t, d)


def kernel(x_prompt, x_sample, norm_mix, norm_ffn, pool_w, pool_b, pool_scale, attn_w_qkv, attn_q_gain, attn_k_gain, attn_w_o, ffn_w_gate_up, ffn_w_down):
    assert x_prompt.shape[1:] == x_sample.shape[1:]
    _, t, d = x_prompt.shape
    n_heads = d // HEAD_DIM
    n_attn = attn_w_qkv.shape[0]
    lanes = _rope_layout()
    gq = attn_q_gain[:, lanes] * (math.log2(math.e) / math.sqrt(HEAD_DIM))
    gk = attn_k_gain[:, lanes]
    tables = _rope_tables(t)
    p = dict(
        norm_mix=norm_mix[:, None, :], norm_ffn=norm_ffn[:, None, :],
        pool_w=pool_w.astype(BF16), pool_b=pool_b.reshape(pool_b.shape[0], 1, d),
        pool_scale=pool_scale[:, None, :],
        attn_w_qkv_t=jax.vmap(lambda w: _qkv_weight_t(w, n_heads))(attn_w_qkv).astype(BF16),
        q_tables=[_gain_rope_tables(gq[j], tables) for j in range(n_attn)],
        k_tables=[_gain_rope_tables(gk[j], tables) for j in range(n_attn)],
        score_bound=[_score_bound(gq[j], gk[j]) for j in range(n_attn)],
        attn_w_o=attn_w_o.astype(BF16),
        ffn_w_gate_up=ffn_w_gate_up.astype(BF16), ffn_w_down=ffn_w_down.astype(BF16),
    )
    return (_trunk(x_prompt, p), _trunk(x_sample, p))
```

```python
import functools
import math

import jax
import jax.numpy as jnp
from jax import lax
from jax.experimental import pallas as pl
from jax.experimental.pallas import tpu as pltpu

F32 = jnp.float32
BF16 = jnp.bfloat16

POOL_WINDOWS = (2, 4, 8, 16)
HEAD_DIM = 128
N_KV_HEADS = 2
ROPE_AXIS_DIM = HEAD_DIM // 2
ROPE_THETA = 10000.0
GRID_W = 64
EPS = 1e-6

V7X_SUBLANES = 8
V7X_VMEM_BYTES = 64 * 1024 * 1024

FFN_ROWS = {"pool": 512, "attn": 1024}
FFN_COL_CHUNK = 256
QKV_ROWS = 1024
QKV_COL_GROUP = 512
ATTN_Q_ROWS = 512
ATTN_KEY_CHUNK = 4096
ATTN_EXP2_HEADROOM = 60.0
ATTN_FIXED_OFFSET_MAX_BOUND = 75.0
BF16_ROUNDING_MARGIN = 1.02
ATTN_STAGES = 3
POOL_HALO = V7X_SUBLANES
VMEM_LIMIT = V7X_VMEM_BYTES - 8 * 1024 * 1024


def _rms(x, g):
    ms = jnp.mean(x * x, axis=-1, keepdims=True)
    return x * lax.rsqrt(ms + EPS) * g


def _layer_spec(stacked, layer):
    tail = stacked.shape[1:]
    return pl.BlockSpec((None,) + tail, lambda *_: (layer,) + (0,) * len(tail),
                        pipeline_mode=pl.Buffered(1))


def _pool_norm(x_ref, before_ref, after_ref, tile, g_mix_ref, hext, *, seq_len):
    rows = x_ref.shape[0]
    tiles_per_seq = seq_len // rows
    kk = tile % tiles_per_seq
    g = g_mix_ref[...]
    hext[POOL_HALO:POOL_HALO + rows, :] = _rms(x_ref[...], g)
    hext[0:POOL_HALO, :] = jnp.where(kk > 0, _rms(before_ref[...], g), 0.0)
    hext[POOL_HALO + rows:, :] = jnp.where(kk < tiles_per_seq - 1, _rms(after_ref[...], g), 0.0)


def _pool_group(gi, x_ref, tile, prm, x1_dst, hext, *, seq_len):
    _, w_ref, b_ref, sc_ref, _ = prm
    rows = x_ref.shape[0]
    gd = w_ref.shape[1]
    kk = tile % (seq_len // rows)
    ext_rows = rows + 2 * POOL_HALO
    edge = V7X_SUBLANES
    t_top = kk * rows + lax.broadcasted_iota(jnp.int32, (edge, 1), 0)
    t_bot = t_top + (rows - edge)
    w = POOL_WINDOWS[gi]
    half = w // 2
    cs = slice(gi * gd, (gi + 1) * gd)

    def count(t):
        return (jnp.minimum(t + half, seq_len) - jnp.maximum(t - half, 0)).astype(F32)

    run, width = hext[:, cs], 1
    while width < half:
        run = run + pltpu.roll(run, ext_rows - width, 0)
        width *= 2
    start = POOL_HALO - half
    wsum = run[start:start + rows] + run[POOL_HALO:POOL_HALO + rows]
    wmean = jnp.concatenate([wsum[:edge] / count(t_top),
                             wsum[edge:rows - edge] * (1.0 / w),
                             wsum[rows - edge:] / count(t_bot)], axis=0)
    pooled = wmean - hext[POOL_HALO:POOL_HALO + rows, cs]
    y = jnp.dot(pooled.astype(BF16), w_ref[gi], preferred_element_type=F32) + b_ref[:, cs]
    x1_dst[:, cs] = x_ref[:, cs] + y * sc_ref[:, cs]


def _attn_prologue(x_ref, o_ref, prm, x1_dst, h_dst):
    wo_ref, g_ffn_ref = prm
    x1_dst[...] = x_ref[...] + jnp.dot(o_ref[...], wo_ref[...], preferred_element_type=F32)
    h_dst[...] = _rms(x1_dst[...], g_ffn_ref[...]).astype(BF16)


def _mix_ffn_kernel(*refs, mode, d_ff, seq_len):
    n_in = {"pool": 5, "attn": 2}[mode]
    n_prm = {"pool": 5, "attn": 2}[mode]
    tile_refs, prm = refs[:n_in], refs[n_in:n_in + n_prm]
    wgu_ref, wd_ref, out_ref = refs[n_in + n_prm:n_in + n_prm + 3]
    h_scr, a_scr = refs[n_in + n_prm + 3:n_in + n_prm + 5]
    g_ffn_ref = prm[-1]
    i = pl.program_id(0)

    n_chunks = d_ff // FFN_COL_CHUNK
    ahead = {}
    if mode == "pool":
        x_first, after_first, x_la, before_la, after_la = tile_refs
        x1_next, hext = refs[-2:]
        n_groups = len(POOL_WINDOWS)

        def next_h():
            h_scr[...] = _rms(x1_next[...], g_ffn_ref[...]).astype(BF16)

        @pl.when(i == 0)
        def _():
            _pool_norm(x_first, after_first, after_first, 0, prm[0], hext, seq_len=seq_len)
            for gi in range(n_groups):
                _pool_group(gi, x_first, 0, prm, x1_next, hext, seq_len=seq_len)
            next_h()

        out_ref[...] = x1_next[...]
        ahead[0] = [functools.partial(_pool_norm, x_la, before_la, after_la, i + 1, prm[0], hext,
                                      seq_len=seq_len)]
        for gi in range(n_groups):
            ahead[2 * gi + 1] = [functools.partial(_pool_group, gi, x_la, i + 1, prm, x1_next, hext,
                                                   seq_len=seq_len)]
        assert max(ahead) < n_chunks
        ahead[n_chunks] = [next_h]
    else:
        _attn_prologue(*tile_refs, prm, out_ref, h_scr)

    for c in range(n_chunks):
        for step in ahead.get(c, ()):
            step()
        lo = c * FFN_COL_CHUNK
        gate = jnp.dot(h_scr[...], wgu_ref[:, lo:lo + FFN_COL_CHUNK], preferred_element_type=F32)
        up = jnp.dot(h_scr[...], wgu_ref[:, d_ff + lo:d_ff + lo + FFN_COL_CHUNK], preferred_element_type=F32)
        a_scr[:, lo:lo + FFN_COL_CHUNK] = (jax.nn.silu(gate) * up).astype(BF16)
    for step in ahead.get(n_chunks, ()):
        step()
    out_ref[...] = out_ref[...] + jnp.dot(a_scr[...], wd_ref[...], preferred_element_type=F32)


def _mix_ffn_layer(mode, x2d, seq_len, mix_args, mix_specs, g_ffn, wgu, wd, layer, o2d=None):
    n, d = x2d.shape
    d_ff = wd.shape[1]
    rows = FFN_ROWS[mode]
    n_tiles = n // rows
    blk = rows // POOL_HALO
    const = dict(pipeline_mode=pl.Buffered(1))
    first = pl.BlockSpec((rows, d), lambda i: (0, 0), **const)
    look = pl.BlockSpec((rows, d), lambda i: (jnp.minimum(i + 1, n_tiles - 1), 0))
    if mode == "pool":
        tile_specs = [first,
                      pl.BlockSpec((POOL_HALO, d), lambda i: (blk, 0), **const),
                      look,
                      pl.BlockSpec((POOL_HALO, d), lambda i: (jnp.minimum(i + 1, n_tiles - 1) * blk - 1, 0)),
                      pl.BlockSpec((POOL_HALO, d), lambda i: (jnp.minimum((i + 2) * blk, n_tiles * blk - 1), 0))]
        tile_args = [x2d] * 5
        extra_scratch = [pltpu.VMEM((rows, d), F32), pltpu.VMEM((rows + 2 * POOL_HALO, d), F32)]
    else:
        tile_specs = [pl.BlockSpec((rows, d), lambda i: (i, 0))] * 2
        tile_args = [x2d, o2d]
        extra_scratch = []
    return pl.pallas_call(
        functools.partial(_mix_ffn_kernel, mode=mode, d_ff=d_ff, seq_len=seq_len),
        grid=(n_tiles,),
        in_specs=tile_specs + mix_specs + [_layer_spec(g_ffn, layer), _layer_spec(wgu, layer),
                                           _layer_spec(wd, layer)],
        out_specs=pl.BlockSpec((rows, d), lambda i: (i, 0)),
        out_shape=jax.ShapeDtypeStruct((n, d), F32),
        scratch_shapes=[pltpu.VMEM((rows, d), BF16), pltpu.VMEM((rows, d_ff), BF16)] + extra_scratch,
        compiler_params=pltpu.CompilerParams(
            dimension_semantics=("arbitrary",), vmem_limit_bytes=VMEM_LIMIT),
        name=mode + "_ffn",
    )(*tile_args, *mix_args, g_ffn, wgu, wd)


def _qkv_kernel(x_ref, g_ref, wt_ref, aq_ref, bq_ref, ak_ref, bk_ref,
                qt_ref, k_ref, vt_ref, h_scr, *, n_heads):
    h_scr[...] = _rms(x_ref[...], g_ref[...]).astype(BF16)
    n_groups = wt_ref.shape[0] // QKV_COL_GROUP
    heads_per_group = QKV_COL_GROUP // HEAD_DIM
    half = HEAD_DIM // 2

    def project(gi):
        return lax.dot_general(wt_ref[gi * QKV_COL_GROUP:(gi + 1) * QKV_COL_GROUP, :], h_scr[...],
                               (((1,), (1,)), ((), ())), preferred_element_type=F32)

    def norm_rope(xt, a_ref, b_ref):
        inv = lax.rsqrt(jnp.mean(xt * xt, axis=0, keepdims=True) + EPS)
        swapped = jnp.concatenate([xt[half:], xt[:half]], axis=0)
        return (xt * a_ref[...] + swapped * b_ref[...]) * inv

    nxt = project(0)
    for gi in range(n_groups):
        cur = nxt
        if gi + 1 < n_groups:
            nxt = project(gi + 1)
        for j in range(heads_per_group):
            hd = gi * heads_per_group + j
            xt = cur[j * HEAD_DIM:(j + 1) * HEAD_DIM, :]
            if hd < n_heads:
                qt_ref[hd] = norm_rope(xt, aq_ref, bq_ref).astype(BF16)
            elif hd < n_heads + N_KV_HEADS:
                k_ref[hd - n_heads] = norm_rope(xt, ak_ref, bk_ref).T.astype(BF16)
            else:
                vt_ref[hd - n_heads - N_KV_HEADS] = xt.astype(BF16)


def _rope_layout():
    quarter = ROPE_AXIS_DIM // 2
    idx = jnp.arange(HEAD_DIM)
    blk = idx // quarter
    return jnp.array([0, 2, 1, 3])[blk] * quarter + idx % quarter


def _rope_tables(t):
    pos = jnp.arange(t, dtype=F32)
    row = jnp.floor(pos / GRID_W)
    col = pos - row * GRID_W
    inv_freq = ROPE_THETA ** (-jnp.arange(0, ROPE_AXIS_DIM, 2, dtype=F32) / ROPE_AXIS_DIM)
    ang_r = inv_freq[:, None] * row[None, :]
    ang_c = inv_freq[:, None] * col[None, :]
    ang = jnp.concatenate([ang_r, ang_c, ang_r, ang_c], axis=0)
    sign = jnp.where(jnp.arange(HEAD_DIM) < HEAD_DIM // 2, -1.0, 1.0)
    return jnp.cos(ang), jnp.sin(ang) * sign[:, None]


def _gain_rope_tables(gain, tables):
    cos, sin = tables
    return gain[:, None] * cos, jnp.roll(gain, HEAD_DIM // 2)[:, None] * sin


def _qkv_weight_t(w, n_heads):
    d, e = w.shape
    n_qk = n_heads + N_KV_HEADS
    wh = w.reshape(d, e // HEAD_DIM, HEAD_DIM)
    qk = jnp.take(wh[:, :n_qk], _rope_layout(), axis=2)
    return jnp.concatenate([qk, wh[:, n_qk:]], axis=1).reshape(d, e).T


def _qkv_layer(x, g, layer, wt, attn_layer, q_tabs, k_tabs):
    bsz, t, d = x.shape
    n_heads = d // HEAD_DIM
    tab_spec = pl.BlockSpec((HEAD_DIM, QKV_ROWS), lambda bi, i: (0, i))
    return pl.pallas_call(
        functools.partial(_qkv_kernel, n_heads=n_heads),
        grid=(bsz, t // QKV_ROWS),
        in_specs=[pl.BlockSpec((None, QKV_ROWS, d), lambda bi, i: (bi, i, 0)),
                  _layer_spec(g, layer), _layer_spec(wt, attn_layer),
                  tab_spec, tab_spec, tab_spec, tab_spec],
        out_specs=[pl.BlockSpec((None, n_heads, HEAD_DIM, QKV_ROWS), lambda bi, i: (bi, 0, 0, i)),
                   pl.BlockSpec((None, N_KV_HEADS, QKV_ROWS, HEAD_DIM), lambda bi, i: (bi, 0, i, 0)),
                   pl.BlockSpec((None, N_KV_HEADS, HEAD_DIM, QKV_ROWS), lambda bi, i: (bi, 0, 0, i))],
        out_shape=[jax.ShapeDtypeStruct((bsz, n_heads, HEAD_DIM, t), BF16),
                   jax.ShapeDtypeStruct((bsz, N_KV_HEADS, t, HEAD_DIM), BF16),
                   jax.ShapeDtypeStruct((bsz, N_KV_HEADS, HEAD_DIM, t), BF16)],
        scratch_shapes=[pltpu.VMEM((QKV_ROWS, d), BF16)],
        compiler_params=pltpu.CompilerParams(
            dimension_semantics=("arbitrary", "arbitrary"), vmem_limit_bytes=VMEM_LIMIT),
        name="qkv",
    )(x, g, wt, *q_tabs, *k_tabs)


def _key_chunk(t):
    return min(ATTN_KEY_CHUNK, t)


def _attn_units(qt_ref, k_ref):
    n_chunks = k_ref.shape[1] // _key_chunk(k_ref.shape[1])
    return [(hd, c) for hd in range(qt_ref.shape[0]) for c in range(n_chunks)], n_chunks


def _chunk(ref, kh, c, axis):
    size = _key_chunk(ref.shape[1 + axis])
    return ref[kh, c * size:(c + 1) * size, :] if axis == 0 else ref[kh, :, c * size:(c + 1) * size]


def _attn_scores(qt_ref, k_ref, group, hd, c):
    return jnp.dot(_chunk(k_ref, hd // group, c, 0), qt_ref[hd],
                   preferred_element_type=F32)


def _attn_store(o_ref, hd, acc, denom):
    o_ref[:, hd * HEAD_DIM:(hd + 1) * HEAD_DIM] = (acc / denom).T.astype(BF16)


def _attn_fixed_offset(offset, qt_ref, k_ref, vt_ref, o_ref, group):
    units, n_chunks = _attn_units(qt_ref, k_ref)

    def probs(u):
        hd, c = units[u]
        e = jnp.exp2(_attn_scores(qt_ref, k_ref, group, hd, c) - offset)
        return e.astype(BF16), jnp.sum(e, axis=0, keepdims=True)

    nxt = probs(0)
    for u, (hd, c) in enumerate(units):
        p, esum = nxt
        if u + 1 < len(units):
            nxt = probs(u + 1)
        pv = jnp.dot(_chunk(vt_ref, hd // group, c, 1), p, preferred_element_type=F32)
        acc, denom = (pv, esum) if c == 0 else (acc + pv, denom + esum)
        if c == n_chunks - 1:
            _attn_store(o_ref, hd, acc, denom)


def _attn_online(qt_ref, k_ref, vt_ref, o_ref, st_scr, group):
    rows = qt_ref.shape[-1]
    units, n_chunks = _attn_units(qt_ref, k_ref)
    n_units = len(units)
    cmax = {}

    def scores(u):
        st_scr[u % ATTN_STAGES] = _attn_scores(qt_ref, k_ref, group, *units[u])

    def chunk_max(u):
        cmax[u] = jnp.max(st_scr[u % ATTN_STAGES], axis=0, keepdims=True)

    scores(0)
    scores(1)
    chunk_max(0)
    for u, (hd, c) in enumerate(units):
        if u + 2 < n_units:
            scores(u + 2)
        if u + 1 < n_units:
            chunk_max(u + 1)
        if c == 0:
            m = jnp.full((1, rows), -jnp.inf, F32)
            denom = jnp.zeros((1, rows), F32)
            acc = jnp.zeros((HEAD_DIM, rows), F32)
        m_new = jnp.maximum(m, cmax.pop(u))
        alpha = jnp.exp2(m - m_new)
        e = jnp.exp2(st_scr[u % ATTN_STAGES] - m_new)
        denom = alpha * denom + jnp.sum(e, axis=0, keepdims=True)
        acc = alpha * acc + jnp.dot(_chunk(vt_ref, hd // group, c, 1), e.astype(BF16),
                                    preferred_element_type=F32)
        m = m_new
        if c == n_chunks - 1:
            _attn_store(o_ref, hd, acc, denom)


def _attn_kernel(bound_ref, qt_ref, k_ref, vt_ref, o_ref, st_scr, *, group):
    bound = bound_ref[0]

    @pl.when(bound <= ATTN_FIXED_OFFSET_MAX_BOUND)
    def _():
        _attn_fixed_offset(bound - ATTN_EXP2_HEADROOM, qt_ref, k_ref, vt_ref, o_ref, group)

    @pl.when(jnp.logical_not(bound <= ATTN_FIXED_OFFSET_MAX_BOUND))
    def _():
        _attn_online(qt_ref, k_ref, vt_ref, o_ref, st_scr, group)


def _score_bound(gq, gk):
    return (HEAD_DIM * BF16_ROUNDING_MARGIN * jnp.max(jnp.abs(gq)) * jnp.max(jnp.abs(gk))).reshape(1)


def _attn_layer(qt, k, vt, bound):
    bsz, n_heads, _, t = qt.shape
    d = n_heads * HEAD_DIM
    group = n_heads // N_KV_HEADS
    return pl.pallas_call(
        functools.partial(_attn_kernel, group=group),
        grid=(bsz, t // ATTN_Q_ROWS),
        in_specs=[pl.BlockSpec(memory_space=pltpu.SMEM),
                  pl.BlockSpec((None, n_heads, HEAD_DIM, ATTN_Q_ROWS), lambda bi, i: (bi, 0, 0, i)),
                  pl.BlockSpec((None, N_KV_HEADS, t, HEAD_DIM), lambda bi, i: (bi, 0, 0, 0)),
                  pl.BlockSpec((None, N_KV_HEADS, HEAD_DIM, t), lambda bi, i: (bi, 0, 0, 0))],
        out_specs=pl.BlockSpec((None, ATTN_Q_ROWS, d), lambda bi, i: (bi, i, 0)),
        out_shape=jax.ShapeDtypeStruct((bsz, t, d), BF16),
        scratch_shapes=[pltpu.VMEM((ATTN_STAGES, _key_chunk(t), ATTN_Q_ROWS), F32)],
        compiler_params=pltpu.CompilerParams(
            dimension_semantics=("arbitrary", "arbitrary"), vmem_limit_bytes=VMEM_LIMIT),
        name="attn",
    )(bound, qt, k, vt)


def _trunk(x, p):
    bsz, t, d = x.shape
    depth = p["norm_mix"].shape[0]
    x2d = x.reshape(bsz * t, d)
    for i in range(depth):
        j = i // 2
        ffn = (p["norm_ffn"], p["ffn_w_gate_up"], p["ffn_w_down"], i)
        if i % 2 == 0:
            mix = [p["norm_mix"], p["pool_w"], p["pool_b"], p["pool_scale"]]
            specs = [_layer_spec(mix[0], i)] + [_layer_spec(m, j) for m in mix[1:]]
            x2d = _mix_ffn_layer("pool", x2d, t, mix, specs, *ffn)
        else:
            qt, k, vt = _qkv_layer(x2d.reshape(bsz, t, d), p["norm_mix"], i, p["attn_w_qkv_t"], j,
                                   p["q_tables"][j], p["k_tables"][j])
            o2d = _attn_layer(qt, k, vt, p["score_bound"][j]).reshape(bsz * t, d)
            x2d = _mix_ffn_layer("attn", x2d, t, [p["attn_w_o"]], [_layer_spec(p["attn_w_o"], j)],
                                 *ffn, o2d=o2d)
    return x2d.reshape(bsz, t, d)


def kernel(x_prompt, x_sample, norm_mix, norm_ffn, pool_w, pool_b, pool_scale, attn_w_qkv, attn_q_gain, attn_k_gain, attn_w_o, ffn_w_gate_up, ffn_w_down):
    assert x_prompt.shape[1:] == x_sample.shape[1:]
    _, t, d = x_prompt.shape
    n_heads = d // HEAD_DIM
    n_attn = attn_w_qkv.shape[0]
    lanes = _rope_layout()
    gq = attn_q_gain[:, lanes] * (math.log2(math.e) / math.sqrt(HEAD_DIM))
    gk = attn_k_gain[:, lanes]
    tables = _rope_tables(t)
    p = dict(
        norm_mix=norm_mix[:, None, :], norm_ffn=norm_ffn[:, None, :],
        pool_w=pool_w.astype(BF16), pool_b=pool_b.reshape(pool_b.shape[0], 1, d),
        pool_scale=pool_scale[:, None, :],
        attn_w_qkv_t=jax.vmap(lambda w: _qkv_weight_t(w, n_heads))(attn_w_qkv).astype(BF16),
        q_tables=[_gain_rope_tables(gq[j], tables) for j in range(n_attn)],
        k_tables=[_gain_rope_tables(gk[j], tables) for j in range(n_attn)],
        score_bound=[_score_bound(gq[j], gk[j]) for j in range(n_attn)],
        attn_w_o=attn_w_o.astype(BF16),
        ffn_w_gate_up=ffn_w_gate_up.astype(BF16), ffn_w_down=ffn_w_down.astype(BF16),
    )
    return (_trunk(x_prompt, p), _trunk(x_sample, p))
```

```python
import functools
import math

import jax
import jax.numpy as jnp
from jax import lax
from jax.experimental import pallas as pl
from jax.experimental.pallas import tpu as pltpu

F32 = jnp.float32
BF16 = jnp.bfloat16

POOL_WINDOWS = (2, 4, 8, 16)
HEAD_DIM = 128
N_KV_HEADS = 2
ROPE_AXIS_DIM = HEAD_DIM // 2
ROPE_THETA = 10000.0
GRID_W = 64
EPS = 1e-6

V7X_SUBLANES = 8
V7X_VMEM_BYTES = 64 * 1024 * 1024

FFN_ROWS = {"pool": 512, "attn": 1024}
FFN_COL_CHUNK = 256
QKV_ROWS = 2048
QKV_COL_GROUP = 512
ATTN_Q_ROWS = 512
ATTN_KEY_CHUNK = 4096
ATTN_EXP2_HEADROOM = 60.0
ATTN_FIXED_OFFSET_MAX_BOUND = 75.0
BF16_ROUNDING_MARGIN = 1.02
ATTN_STAGES = 3
POOL_HALO = V7X_SUBLANES
VMEM_LIMIT = V7X_VMEM_BYTES - 8 * 1024 * 1024


def _rms(x, g):
    ms = jnp.mean(x * x, axis=-1, keepdims=True)
    return x * lax.rsqrt(ms + EPS) * g


def _layer_spec(stacked, layer):
    tail = stacked.shape[1:]
    return pl.BlockSpec((None,) + tail, lambda *_: (layer,) + (0,) * len(tail),
                        pipeline_mode=pl.Buffered(1))


def _pool_norm(x_ref, before_ref, after_ref, tile, g_mix_ref, hext, *, seq_len):
    rows = x_ref.shape[0]
    tiles_per_seq = seq_len // rows
    kk = tile % tiles_per_seq
    g = g_mix_ref[...]
    hext[POOL_HALO:POOL_HALO + rows, :] = _rms(x_ref[...], g)
    hext[0:POOL_HALO, :] = jnp.where(kk > 0, _rms(before_ref[...], g), 0.0)
    hext[POOL_HALO + rows:, :] = jnp.where(kk < tiles_per_seq - 1, _rms(after_ref[...], g), 0.0)


def _pool_group(gi, x_ref, tile, prm, x1_dst, hext, *, seq_len):
    _, w_ref, b_ref, sc_ref, _ = prm
    rows = x_ref.shape[0]
    gd = w_ref.shape[1]
    kk = tile % (seq_len // rows)
    ext_rows = rows + 2 * POOL_HALO
    edge = V7X_SUBLANES
    t_top = kk * rows + lax.broadcasted_iota(jnp.int32, (edge, 1), 0)
    t_bot = t_top + (rows - edge)
    w = POOL_WINDOWS[gi]
    half = w // 2
    cs = slice(gi * gd, (gi + 1) * gd)

    def count(t):
        return (jnp.minimum(t + half, seq_len) - jnp.maximum(t - half, 0)).astype(F32)

    run, width = hext[:, cs], 1
    while width < half:
        run = run + pltpu.roll(run, ext_rows - width, 0)
        width *= 2
    start = POOL_HALO - half
    wsum = run[start:start + rows] + run[POOL_HALO:POOL_HALO + rows]
    wmean = jnp.concatenate([wsum[:edge] / count(t_top),
                             wsum[edge:rows - edge] * (1.0 / w),
                             wsum[rows - edge:] / count(t_bot)], axis=0)
    pooled = wmean - hext[POOL_HALO:POOL_HALO + rows, cs]
    y = jnp.dot(pooled.astype(BF16), w_ref[gi], preferred_element_type=F32) + b_ref[:, cs]
    x1_dst[:, cs] = x_ref[:, cs] + y * sc_ref[:, cs]


def _attn_prologue(x_ref, o_ref, prm, x1_dst, h_dst):
    wo_ref, g_ffn_ref = prm
    x1_dst[...] = x_ref[...] + jnp.dot(o_ref[...], wo_ref[...], preferred_element_type=F32)
    h_dst[...] = _rms(x1_dst[...], g_ffn_ref[...]).astype(BF16)


def _mix_ffn_kernel(*refs, mode, d_ff, seq_len):
    n_in = {"pool": 5, "attn": 2}[mode]
    n_prm = {"pool": 5, "attn": 2}[mode]
    tile_refs, prm = refs[:n_in], refs[n_in:n_in + n_prm]
    wgu_ref, wd_ref, out_ref = refs[n_in + n_prm:n_in + n_prm + 3]
    h_scr, a_scr = refs[n_in + n_prm + 3:n_in + n_prm + 5]
    g_ffn_ref = prm[-1]
    i = pl.program_id(0)

    n_chunks = d_ff // FFN_COL_CHUNK
    ahead = {}
    if mode == "pool":
        x_first, after_first, x_la, before_la, after_la = tile_refs
        x1_next, hext = refs[-2:]
        n_groups = len(POOL_WINDOWS)

        def next_h():
            h_scr[...] = _rms(x1_next[...], g_ffn_ref[...]).astype(BF16)

        @pl.when(i == 0)
        def _():
            _pool_norm(x_first, after_first, after_first, 0, prm[0], hext, seq_len=seq_len)
            for gi in range(n_groups):
                _pool_group(gi, x_first, 0, prm, x1_next, hext, seq_len=seq_len)
            next_h()

        out_ref[...] = x1_next[...]
        ahead[0] = [functools.partial(_pool_norm, x_la, before_la, after_la, i + 1, prm[0], hext,
                                      seq_len=seq_len)]
        for gi in range(n_groups):
            ahead[2 * gi + 1] = [functools.partial(_pool_group, gi, x_la, i + 1, prm, x1_next, hext,
                                                   seq_len=seq_len)]
        assert max(ahead) < n_chunks
        ahead[n_chunks] = [next_h]
    else:
        _attn_prologue(*tile_refs, prm, out_ref, h_scr)

    for c in range(n_chunks):
        for step in ahead.get(c, ()):
            step()
        lo = c * FFN_COL_CHUNK
        gate = jnp.dot(h_scr[...], wgu_ref[:, lo:lo + FFN_COL_CHUNK], preferred_element_type=F32)
        up = jnp.dot(h_scr[...], wgu_ref[:, d_ff + lo:d_ff + lo + FFN_COL_CHUNK], preferred_element_type=F32)
        a_scr[:, lo:lo + FFN_COL_CHUNK] = (jax.nn.silu(gate) * up).astype(BF16)
    for step in ahead.get(n_chunks, ()):
        step()
    out_ref[...] = out_ref[...] + jnp.dot(a_scr[...], wd_ref[...], preferred_element_type=F32)


def _mix_ffn_layer(mode, x2d, seq_len, mix_args, mix_specs, g_ffn, wgu, wd, layer, o2d=None):
    n, d = x2d.shape
    d_ff = wd.shape[1]
    rows = FFN_ROWS[mode]
    n_tiles = n // rows
    blk = rows // POOL_HALO
    const = dict(pipeline_mode=pl.Buffered(1))
    first = pl.BlockSpec((rows, d), lambda i: (0, 0), **const)
    look = pl.BlockSpec((rows, d), lambda i: (jnp.minimum(i + 1, n_tiles - 1), 0))
    if mode == "pool":
        tile_specs = [first,
                      pl.BlockSpec((POOL_HALO, d), lambda i: (blk, 0), **const),
                      look,
                      pl.BlockSpec((POOL_HALO, d), lambda i: (jnp.minimum(i + 1, n_tiles - 1) * blk - 1, 0)),
                      pl.BlockSpec((POOL_HALO, d), lambda i: (jnp.minimum((i + 2) * blk, n_tiles * blk - 1), 0))]
        tile_args = [x2d] * 5
        extra_scratch = [pltpu.VMEM((rows, d), F32), pltpu.VMEM((rows + 2 * POOL_HALO, d), F32)]
    else:
        tile_specs = [pl.BlockSpec((rows, d), lambda i: (i, 0))] * 2
        tile_args = [x2d, o2d]
        extra_scratch = []
    return pl.pallas_call(
        functools.partial(_mix_ffn_kernel, mode=mode, d_ff=d_ff, seq_len=seq_len),
        grid=(n_tiles,),
        in_specs=tile_specs + mix_specs + [_layer_spec(g_ffn, layer), _layer_spec(wgu, layer),
                                           _layer_spec(wd, layer)],
        out_specs=pl.BlockSpec((rows, d), lambda i: (i, 0)),
        out_shape=jax.ShapeDtypeStruct((n, d), F32),
        scratch_shapes=[pltpu.VMEM((rows, d), BF16), pltpu.VMEM((rows, d_ff), BF16)] + extra_scratch,
        compiler_params=pltpu.CompilerParams(
            dimension_semantics=("arbitrary",), vmem_limit_bytes=VMEM_LIMIT),
        name=mode + "_ffn",
    )(*tile_args, *mix_args, g_ffn, wgu, wd)


def _qkv_kernel(x_ref, g_ref, wt_ref, aq_ref, bq_ref, ak_ref, bk_ref,
                qt_ref, k_ref, vt_ref, h_scr, *, n_heads):
    h_scr[...] = _rms(x_ref[...], g_ref[...]).astype(BF16)
    n_groups = wt_ref.shape[0] // QKV_COL_GROUP
    heads_per_group = QKV_COL_GROUP // HEAD_DIM
    half = HEAD_DIM // 2

    def project(gi):
        return lax.dot_general(wt_ref[gi * QKV_COL_GROUP:(gi + 1) * QKV_COL_GROUP, :], h_scr[...],
                               (((1,), (1,)), ((), ())), preferred_element_type=F32)

    def norm_rope(xt, a_ref, b_ref):
        inv = lax.rsqrt(jnp.mean(xt * xt, axis=0, keepdims=True) + EPS)
        swapped = jnp.concatenate([xt[half:], xt[:half]], axis=0)
        return (xt * a_ref[...] + swapped * b_ref[...]) * inv

    nxt = project(0)
    for gi in range(n_groups):
        cur = nxt
        if gi + 1 < n_groups:
            nxt = project(gi + 1)
        for j in range(heads_per_group):
            hd = gi * heads_per_group + j
            xt = cur[j * HEAD_DIM:(j + 1) * HEAD_DIM, :]
            if hd < n_heads:
                qt_ref[hd] = norm_rope(xt, aq_ref, bq_ref).astype(BF16)
            elif hd < n_heads + N_KV_HEADS:
                k_ref[hd - n_heads] = norm_rope(xt, ak_ref, bk_ref).T.astype(BF16)
            else:
                vt_ref[hd - n_heads - N_KV_HEADS] = xt.astype(BF16)


def _rope_layout():
    quarter = ROPE_AXIS_DIM // 2
    idx = jnp.arange(HEAD_DIM)
    blk = idx // quarter
    return jnp.array([0, 2, 1, 3])[blk] * quarter + idx % quarter


def _rope_tables(t):
    pos = jnp.arange(t, dtype=F32)
    row = jnp.floor(pos / GRID_W)
    col = pos - row * GRID_W
    inv_freq = ROPE_THETA ** (-jnp.arange(0, ROPE_AXIS_DIM, 2, dtype=F32) / ROPE_AXIS_DIM)
    ang_r = inv_freq[:, None] * row[None, :]
    ang_c = inv_freq[:, None] * col[None, :]
    ang = jnp.concatenate([ang_r, ang_c, ang_r, ang_c], axis=0)
    sign = jnp.where(jnp.arange(HEAD_DIM) < HEAD_DIM // 2, -1.0, 1.0)
    return jnp.cos(ang), jnp.sin(ang) * sign[:, None]


def _gain_rope_tables(gain, tables):
    cos, sin = tables
    return gain[:, None] * cos, jnp.roll(gain, HEAD_DIM // 2)[:, None] * sin


def _qkv_weight_t(w, n_heads):
    d, e = w.shape
    n_qk = n_heads + N_KV_HEADS
    wh = w.reshape(d, e // HEAD_DIM, HEAD_DIM)
    qk = jnp.take(wh[:, :n_qk], _rope_layout(), axis=2)
    return jnp.concatenate([qk, wh[:, n_qk:]], axis=1).reshape(d, e).T


def _qkv_layer(x, g, layer, wt, attn_layer, q_tabs, k_tabs):
    bsz, t, d = x.shape
    n_heads = d // HEAD_DIM
    rows = min(QKV_ROWS, t)
    tab_spec = pl.BlockSpec((HEAD_DIM, rows), lambda bi, i: (0, i))
    return pl.pallas_call(
        functools.partial(_qkv_kernel, n_heads=n_heads),
        grid=(bsz, t // rows),
        in_specs=[pl.BlockSpec((None, rows, d), lambda bi, i: (bi, i, 0)),
                  _layer_spec(g, layer), _layer_spec(wt, attn_layer),
                  tab_spec, tab_spec, tab_spec, tab_spec],
        out_specs=[pl.BlockSpec((None, n_heads, HEAD_DIM, rows), lambda bi, i: (bi, 0, 0, i)),
                   pl.BlockSpec((None, N_KV_HEADS, rows, HEAD_DIM), lambda bi, i: (bi, 0, i, 0)),
                   pl.BlockSpec((None, N_KV_HEADS, HEAD_DIM, rows), lambda bi, i: (bi, 0, 0, i))],
        out_shape=[jax.ShapeDtypeStruct((bsz, n_heads, HEAD_DIM, t), BF16),
                   jax.ShapeDtypeStruct((bsz, N_KV_HEADS, t, HEAD_DIM), BF16),
                   jax.ShapeDtypeStruct((bsz, N_KV_HEADS, HEAD_DIM, t), BF16)],
        scratch_shapes=[pltpu.VMEM((rows, d), BF16)],
        compiler_params=pltpu.CompilerParams(
            dimension_semantics=("arbitrary", "arbitrary"), vmem_limit_bytes=VMEM_LIMIT),
        name="qkv",
    )(x, g, wt, *q_tabs, *k_tabs)


def _key_chunk(t):
    return min(ATTN_KEY_CHUNK, t)


def _attn_units(k_ref, group):
    n_chunks = k_ref.shape[0] // _key_chunk(k_ref.shape[0])
    return [(hd, c) for hd in range(group) for c in range(n_chunks)], n_chunks


def _chunk(ref, c, axis):
    size = _key_chunk(ref.shape[axis])
    return ref[c * size:(c + 1) * size, :] if axis == 0 else ref[:, c * size:(c + 1) * size]


def _attn_scores(qt_ref, k_ref, hd, c):
    return jnp.dot(_chunk(k_ref, c, 0), qt_ref[hd], preferred_element_type=F32)


def _attn_store(o_ref, hd, acc, denom):
    o_ref[:, hd * HEAD_DIM:(hd + 1) * HEAD_DIM] = (acc / denom).T.astype(BF16)


def _attn_fixed_offset(offset, qt_ref, k_ref, vt_ref, o_ref, group):
    units, n_chunks = _attn_units(k_ref, group)

    def probs(u):
        hd, c = units[u]
        e = jnp.exp2(_attn_scores(qt_ref, k_ref, hd, c) - offset)
        return e.astype(BF16), jnp.sum(e, axis=0, keepdims=True)

    nxt = probs(0)
    for u, (hd, c) in enumerate(units):
        p, esum = nxt
        if u + 1 < len(units):
            nxt = probs(u + 1)
        pv = jnp.dot(_chunk(vt_ref, c, 1), p, preferred_element_type=F32)
        acc, denom = (pv, esum) if c == 0 else (acc + pv, denom + esum)
        if c == n_chunks - 1:
            _attn_store(o_ref, hd, acc, denom)


def _attn_online(qt_ref, k_ref, vt_ref, o_ref, st_scr, group):
    rows = qt_ref.shape[-1]
    units, n_chunks = _attn_units(k_ref, group)
    n_units = len(units)
    cmax = {}

    def scores(u):
        st_scr[u % ATTN_STAGES] = _attn_scores(qt_ref, k_ref, *units[u])

    def chunk_max(u):
        cmax[u] = jnp.max(st_scr[u % ATTN_STAGES], axis=0, keepdims=True)

    scores(0)
    scores(1)
    chunk_max(0)
    for u, (hd, c) in enumerate(units):
        if u + 2 < n_units:
            scores(u + 2)
        if u + 1 < n_units:
            chunk_max(u + 1)
        if c == 0:
            m = jnp.full((1, rows), -jnp.inf, F32)
            denom = jnp.zeros((1, rows), F32)
            acc = jnp.zeros((HEAD_DIM, rows), F32)
        m_new = jnp.maximum(m, cmax.pop(u))
        alpha = jnp.exp2(m - m_new)
        e = jnp.exp2(st_scr[u % ATTN_STAGES] - m_new)
        denom = alpha * denom + jnp.sum(e, axis=0, keepdims=True)
        acc = alpha * acc + jnp.dot(_chunk(vt_ref, c, 1), e.astype(BF16), preferred_element_type=F32)
        m = m_new
        if c == n_chunks - 1:
            _attn_store(o_ref, hd, acc, denom)


def _attn_kernel(bound_ref, qt_ref, k_ref, vt_ref, o_ref, st_scr, *, group):
    bound = bound_ref[0]

    @pl.when(bound <= ATTN_FIXED_OFFSET_MAX_BOUND)
    def _():
        _attn_fixed_offset(bound - ATTN_EXP2_HEADROOM, qt_ref, k_ref, vt_ref, o_ref, group)

    @pl.when(jnp.logical_not(bound <= ATTN_FIXED_OFFSET_MAX_BOUND))
    def _():
        _attn_online(qt_ref, k_ref, vt_ref, o_ref, st_scr, group)


def _score_bound(gq, gk):
    return (HEAD_DIM * BF16_ROUNDING_MARGIN * jnp.max(jnp.abs(gq)) * jnp.max(jnp.abs(gk))).reshape(1)


def _attn_layer(qt, k, vt, bound):
    bsz, n_heads, _, t = qt.shape
    d = n_heads * HEAD_DIM
    group = n_heads // N_KV_HEADS
    gw = group * HEAD_DIM
    o_spec = pl.BlockSpec((None, ATTN_Q_ROWS, gw), lambda bi, kh, i: (bi, i, kh))
    return pl.pallas_call(
        functools.partial(_attn_kernel, group=group),
        grid=(bsz, N_KV_HEADS, t // ATTN_Q_ROWS),
        in_specs=[pl.BlockSpec(memory_space=pltpu.SMEM),
                  pl.BlockSpec((None, group, HEAD_DIM, ATTN_Q_ROWS), lambda bi, kh, i: (bi, kh, 0, i)),
                  pl.BlockSpec((None, None, t, HEAD_DIM), lambda bi, kh, i: (bi, kh, 0, 0)),
                  pl.BlockSpec((None, None, HEAD_DIM, t), lambda bi, kh, i: (bi, kh, 0, 0))],
        out_specs=o_spec,
        out_shape=jax.ShapeDtypeStruct((bsz, t, d), BF16),
        scratch_shapes=[pltpu.VMEM((ATTN_STAGES, _key_chunk(t), ATTN_Q_ROWS), F32)],
        compiler_params=pltpu.CompilerParams(
            dimension_semantics=("arbitrary", "arbitrary", "arbitrary"), vmem_limit_bytes=VMEM_LIMIT),
        name="attn",
    )(bound, qt, k, vt)


def _trunk(x, p):
    bsz, t, d = x.shape
    depth = p["norm_mix"].shape[0]
    x2d = x.reshape(bsz * t, d)
    for i in range(depth):
        j = i // 2
        ffn = (p["norm_ffn"], p["ffn_w_gate_up"], p["ffn_w_down"], i)
        if i % 2 == 0:
            mix = [p["norm_mix"], p["pool_w"], p["pool_b"], p["pool_scale"]]
            specs = [_layer_spec(mix[0], i)] + [_layer_spec(m, j) for m in mix[1:]]
            x2d = _mix_ffn_layer("pool", x2d, t, mix, specs, *ffn)
        else:
            qt, k, vt = _qkv_layer(x2d.reshape(bsz, t, d), p["norm_mix"], i, p["attn_w_qkv_t"], j,
                                   p["q_tables"][j], p["k_tables"][j])
            o2d = _attn_layer(qt, k, vt, p["score_bound"][j]).reshape(bsz * t, d)
            x2d = _mix_ffn_layer("attn", x2d, t, [p["attn_w_o"]], [_layer_spec(p["attn_w_o"], j)],
                                 *ffn, o2d=o2d)
    return x2d.reshape(bsz, t, d)


def kernel(x_prompt, x_sample, norm_mix, norm_ffn, pool_w, pool_b, pool_scale, attn_w_qkv, attn_q_gain, attn_k_gain, attn_w_o, ffn_w_gate_up, ffn_w_down):
    assert x_prompt.shape[1:] == x_sample.shape[1:]
    _, t, d = x_prompt.shape
    n_heads = d // HEAD_DIM
    n_attn = attn_w_qkv.shape[0]
    lanes = _rope_layout()
    gq = attn_q_gain[:, lanes] * (math.log2(math.e) / math.sqrt(HEAD_DIM))
    gk = attn_k_gain[:, lanes]
    tables = _rope_tables(t)
    p = dict(
        norm_mix=norm_mix[:, None, :], norm_ffn=norm_ffn[:, None, :],
        pool_w=pool_w.astype(BF16), pool_b=pool_b.reshape(pool_b.shape[0], 1, d),
        pool_scale=pool_scale[:, None, :],
        attn_w_qkv_t=jax.vmap(lambda w: _qkv_weight_t(w, n_heads))(attn_w_qkv).astype(BF16),
        q_tables=[_gain_rope_tables(gq[j], tables) for j in range(n_attn)],
        k_tables=[_gain_rope_tables(gk[j], tables) for j in range(n_attn)],
        score_bound=[_score_bound(gq[j], gk[j]) for j in range(n_attn)],
        attn_w_o=attn_w_o.astype(BF16),
        ffn_w_gate_up=ffn_w_gate_up.astype(BF16), ffn_w_down=ffn_w_down.astype(BF16),
    )
    return (_trunk(x_prompt, p), _trunk(x_sample, p))
```
